```python
import math
import jax
import jax.numpy as jnp
from jax import lax
import numpy as np

D_MODEL = 2048
BATCH = 32
SEQ = 256
DEPTH = 2
DEC_BATCH = 2
DEC_SEQ = 4096
PAST_LEN = 256

GRID_W = 64
HEAD_DIM = 128
HY_WIDTH = 512
HY_ORDER = 2
HY_IN = (HY_ORDER + 1) * HY_WIDTH
HY_POS_EMB = 33
HY_FILT_HID = 64
HY_DECAY_TARGET = 1e-2
HY_FAST_PCT = 0.3
HY_SLOW_PCT = 1.5
NA_HEADS = 6
NA_WIN_H = 8
NA_WIN_W = 16
GQA_Q_HEADS = 6
GQA_KV_HEADS = 2
ROPE_THETA = 10000.0
MIX_WIDTH = HY_WIDTH + (NA_HEADS + GQA_Q_HEADS) * HEAD_DIM
IN_WIDTH = HY_IN + 3 * NA_HEADS * HEAD_DIM + (GQA_Q_HEADS + 2 * GQA_KV_HEADS) * HEAD_DIM
N_EXPERTS = 16
EC_CAPACITY_FACTOR = 2
EXPERT_FF = 2048
Q_BLOCK = 128
NORM_EPS = 1e-6
MASK_VALUE = -1e30

kernel_name = 'hybrid_hyena_natten_gqa_ec_diffusion_step'


def _rms(x, g):
    xf = x.astype(jnp.float32)
    y = xf * lax.rsqrt(jnp.mean(xf * xf, axis=-1, keepdims=True) + NORM_EPS)
    return (y * g.astype(jnp.float32)).astype(x.dtype)


def _modulate(x, g, shift, scale):
    return _rms(x, g) * (1 + scale[:, None, :]) + shift[:, None, :]


def _heads(t, n_heads):
    b, l, _ = t.shape
    return t.reshape(b, l, n_heads, HEAD_DIM).transpose(0, 2, 1, 3)


def _merge_heads(t):
    b, h, l, d = t.shape
    return t.transpose(0, 2, 1, 3).reshape(b, l, h * d)


def _axial_rope(x):
    n = x.shape[2]
    pos = jnp.arange(n)
    row = (pos // GRID_W).astype(jnp.float32)
    col = (pos % GRID_W).astype(jnp.float32)
    half = HEAD_DIM // 2
    quarter = half // 2
    inv = ROPE_THETA ** (-jnp.arange(quarter, dtype=jnp.float32) / quarter)
    xf = x.astype(jnp.float32)

    def rot(xa, p):
        ang = p[:, None] * inv[None, :]
        cos, sin = jnp.cos(ang), jnp.sin(ang)
        a, b = xa[..., :quarter], xa[..., quarter:]
        return jnp.concatenate([a * cos - b * sin, a * sin + b * cos], axis=-1)

    out = jnp.concatenate([rot(xf[..., :half], row), rot(xf[..., half:], col)], axis=-1)
    return out.astype(x.dtype)


def _attend_blocks(q, k, v):
    b, hq, lq, d = q.shape
    hk = k.shape[1]
    g = hq // hk
    nb = lq // Q_BLOCK
    qb = q.reshape(b, hk, g, nb, Q_BLOCK, d).transpose(3, 0, 1, 2, 4, 5)
    scale = d ** -0.5

    def one(qi):
        s = jnp.einsum('bkgqd,bksd->bkgqs', qi, k, preferred_element_type=jnp.float32) * scale
        p = jax.nn.softmax(s, axis=-1)
        o = jnp.einsum('bkgqs,bksd->bkgqd', p.astype(v.dtype), v, preferred_element_type=jnp.float32)
        return o.astype(q.dtype)

    o = lax.map(one, qb)
    return o.transpose(1, 2, 3, 0, 4, 5).reshape(b, hq, lq, d)


def _na_latent(q, k, v, kc, vc, rpb):
    b, h, n, d = q.shape
    rows = n // GRID_W
    kh = min(NA_WIN_H, rows)
    kw = NA_WIN_W
    scale = d ** -0.5
    qg = q.reshape(b, h, rows, GRID_W, d).transpose(2, 0, 1, 3, 4)
    kg = k.reshape(b, h, rows, GRID_W, d)
    vg = v.reshape(b, h, rows, GRID_W, d)
    cols = jnp.arange(GRID_W)
    cstart = jnp.clip(cols - kw // 2, 0, GRID_W - kw)
    col_mask = (cols[None, :] >= cstart[:, None]) & (cols[None, :] < cstart[:, None] + kw)
    col_idx = jnp.clip(cols[None, :] - cols[:, None] + NA_WIN_W - 1, 0, 2 * NA_WIN_W - 2)

    def one(args):
        r, qr = args
        rs = jnp.clip(r - kh // 2, 0, rows - kh)
        kb = lax.dynamic_slice_in_dim(kg, rs, kh, axis=2)
        vb = lax.dynamic_slice_in_dim(vg, rs, kh, axis=2)
        row_idx = rs + jnp.arange(kh) - r + NA_WIN_H - 1
        bias = rpb[:, row_idx[:, None, None], col_idx[None, :, :]]
        s_loc = jnp.einsum('bhqd,bhrkd->bhqrk', qr, kb, preferred_element_type=jnp.float32) * scale
        s_loc = s_loc + bias.transpose(0, 2, 1, 3).astype(jnp.float32)[None]
        s_loc = jnp.where(col_mask[None, None, :, None, :], s_loc, MASK_VALUE)
        s_loc = s_loc.reshape(b, h, GRID_W, kh * GRID_W)
        s_ctx = jnp.einsum('bhqd,bhsd->bhqs', qr, kc, preferred_element_type=jnp.float32) * scale
        p = jax.nn.softmax(jnp.concatenate([s_loc, s_ctx], axis=-1), axis=-1)
        p_loc = p[..., :kh * GRID_W].reshape(b, h, GRID_W, kh, GRID_W).astype(v.dtype)
        p_ctx = p[..., kh * GRID_W:].astype(v.dtype)
        o = (jnp.einsum('bhqrk,bhrkd->bhqd', p_loc, vb, preferred_element_type=jnp.float32)
             + jnp.einsum('bhqs,bhsd->bhqd', p_ctx, vc, preferred_element_type=jnp.float32))
        return o.astype(q.dtype)

    o = lax.map(one, (jnp.arange(rows), qg))
    return o.transpose(1, 2, 0, 3, 4).reshape(b, h, n, d)


def _hyena_filters(l, w1, b1, w2, b2, w3, freq):
    t = jnp.linspace(0.0, 1.0, l, dtype=jnp.float32)[:, None]
    bands = (HY_POS_EMB - 1) // 2
    fr = jnp.linspace(1e-4, bands - 1, bands, dtype=jnp.float32)[None, :]
    w = 2.0 * math.pi * jnp.arange(l, dtype=jnp.float32)[:, None] / l
    z = jnp.concatenate([t, jnp.cos(fr * w), -jnp.sin(fr * w)], axis=-1)
    sf = freq.astype(jnp.float32)
    hid = jnp.sin(sf * (z @ w1.astype(jnp.float32) + b1.astype(jnp.float32)))
    hid = jnp.sin(sf * (hid @ w2.astype(jnp.float32) + b2.astype(jnp.float32)))
    filt = (hid @ w3.astype(jnp.float32)).reshape(l, 2, HY_ORDER, HY_WIDTH)
    min_d = abs(math.log(HY_DECAY_TARGET) / HY_SLOW_PCT)
    max_d = abs(math.log(HY_DECAY_TARGET) / HY_FAST_PCT)
    deltas = jnp.linspace(min_d, max_d, HY_WIDTH, dtype=jnp.float32)
    filt = filt * jnp.exp(-t.reshape(l, 1, 1, 1) * deltas)
    fwd, bwd = filt[:, 0], filt[:, 1]
    k = jnp.concatenate([fwd, jnp.zeros_like(fwd[:1]), bwd[:0:-1]], axis=0)
    return k / jnp.sum(jnp.abs(k), axis=0, keepdims=True)


def _hyena(u, short_w, short_b, w1, b1, w2, b2, w3, freq, bias):
    l = u.shape[1]
    up = jnp.pad(u, ((0, 0), (1, 1), (0, 0)))
    u = up[:, :-2] * short_w[0] + up[:, 1:-1] * short_w[1] + up[:, 2:] * short_w[2] + short_b
    parts = jnp.split(u, HY_ORDER + 1, axis=-1)
    kf = jnp.fft.rfft(_hyena_filters(l, w1, b1, w2, b2, w3, freq), n=2 * l, axis=0)
    z = parts[0].astype(jnp.float32)
    for o in range(HY_ORDER):
        zf = jnp.fft.rfft(z, n=2 * l, axis=1)
        conv = jnp.fft.irfft(zf * kf[None, :, o], n=2 * l, axis=1)[:, :l]
        z = parts[o + 1].astype(jnp.float32) * (conv + z * bias[o].astype(jnp.float32))
    return z.astype(u.dtype)


def _ec_moe(h, router_w, w_gate, w_up, w_down):
    b, n, d = h.shape
    cap = EC_CAPACITY_FACTOR * n // N_EXPERTS
    aff = jax.nn.softmax(jnp.einsum('bnd,de->bne', h, router_w, preferred_element_type=jnp.float32), axis=-1)
    gate, idx = lax.top_k(aff.transpose(0, 2, 1), cap)
    xg = jax.vmap(lambda hb, ib: hb[ib])(h, idx)
    a = jnp.einsum('becd,edf->becf', xg, w_gate)
    up = jnp.einsum('becd,edf->becf', xg, w_up)
    y = jnp.einsum('becf,efd->becd', jax.nn.silu(a) * up, w_down) * gate[..., None].astype(h.dtype)
    return jax.vmap(lambda ib, yb: jnp.zeros((n, d), h.dtype).at[ib.reshape(-1)].add(yb.reshape(-1, d)))(idx, y)


def _layer(x, mod, lp, ctx_kv):
    shift1, scale1, gate1, shift2, scale2, gate2 = jnp.split(mod, 6, axis=-1)
    h = _modulate(x, lp['norm1_g'], shift1, scale1)
    z = jnp.einsum('bld,de->ble', h, lp['w_in'])
    sizes = (HY_IN, NA_HEADS * HEAD_DIM, NA_HEADS * HEAD_DIM, NA_HEADS * HEAD_DIM,
             GQA_Q_HEADS * HEAD_DIM, GQA_KV_HEADS * HEAD_DIM, GQA_KV_HEADS * HEAD_DIM)
    cuts = [int(s) for s in np.cumsum(sizes)[:-1]]
    hy_in, na_q, na_k, na_v, g_q, g_k, g_v = jnp.split(z, cuts, axis=-1)

    y_hy = _hyena(hy_in, lp['hy_short_w'], lp['hy_short_b'], lp['hy_f_w1'], lp['hy_f_b1'],
                  lp['hy_f_w2'], lp['hy_f_b2'], lp['hy_f_w3'], lp['hy_f_freq'], lp['hy_bias'])

    na_q = _rms(_heads(na_q, NA_HEADS), lp['na_q_g'])
    na_k = _rms(_heads(na_k, NA_HEADS), lp['na_k_g'])
    na_v = _heads(na_v, NA_HEADS)
    g_q = _rms(_heads(g_q, GQA_Q_HEADS), lp['gqa_q_g'])
    g_k = _rms(_heads(g_k, GQA_KV_HEADS), lp['gqa_k_g'])
    g_v = _heads(g_v, GQA_KV_HEADS)

    if ctx_kv is None:
        o_na = _attend_blocks(na_q, na_k, na_v)
        o_g = _attend_blocks(g_q, g_k, g_v)
        kv = (na_k, na_v, g_k, g_v)
    else:
        na_kc, na_vc, g_kc, g_vc = ctx_kv
        o_na = _na_latent(na_q, na_k, na_v, na_kc, na_vc, lp['na_rpb'])
        g_q = _axial_rope(g_q)
        g_k = _axial_rope(g_k)
        o_g = _attend_blocks(g_q, jnp.concatenate([g_k, g_kc], axis=2), jnp.concatenate([g_v, g_vc], axis=2))
        kv = None

    mix = jnp.concatenate([y_hy, _merge_heads(o_na), _merge_heads(o_g)], axis=-1)
    x = x + gate1[:, None, :] * jnp.einsum('blm,md->bld', mix, lp['w_out'])
    h2 = _modulate(x, lp['norm2_g'], shift2, scale2)
    x = x + gate2[:, None, :] * _ec_moe(h2, lp['router_w'], lp['exp_w_gate'], lp['exp_w_up'], lp['exp_w_down'])
    return x, kv


def setup_inputs(seed: int = 0) -> dict:
    key = jax.random.key(seed)
    ks = iter(jax.random.split(key, 40))
    D = D_MODEL

    def nrm(shape, s):
        return jax.random.normal(next(ks), shape, jnp.float32) * s

    inp = {}
    inp['x_prompt'] = nrm((BATCH, SEQ, D), 1.0)
    inp['x_sample'] = nrm((DEC_BATCH, DEC_SEQ, D), 1.0)
    inp['cache_na_k'] = nrm((DEC_BATCH, DEPTH, NA_HEADS, PAST_LEN, HEAD_DIM), 1.0)
    inp['cache_na_v'] = nrm((DEC_BATCH, DEPTH, NA_HEADS, PAST_LEN, HEAD_DIM), 1.0)
    inp['cache_gqa_k'] = nrm((DEC_BATCH, DEPTH, GQA_KV_HEADS, PAST_LEN, HEAD_DIM), 1.0)
    inp['cache_gqa_v'] = nrm((DEC_BATCH, DEPTH, GQA_KV_HEADS, PAST_LEN, HEAD_DIM), 1.0)
    inp['c'] = nrm((DEC_BATCH, D), 1.0)
    inp['c_ctx'] = nrm((D,), 1.0)
    inp['ada_w'] = nrm((DEPTH, D, 6 * D), D ** -0.5)
    inp['ada_b'] = nrm((DEPTH, 6 * D), 0.02)
    inp['norm1_g'] = 1.0 + nrm((DEPTH, D), 0.02)
    inp['norm2_g'] = 1.0 + nrm((DEPTH, D), 0.02)
    inp['w_in'] = nrm((DEPTH, D, IN_WIDTH), D ** -0.5)
    inp['w_out'] = nrm((DEPTH, MIX_WIDTH, D), MIX_WIDTH ** -0.5)
    inp['hy_short_w'] = nrm((DEPTH, 3, HY_IN), 3 ** -0.5)
    inp['hy_short_b'] = nrm((DEPTH, HY_IN), 0.02)
    inp['hy_f_w1'] = nrm((DEPTH, HY_POS_EMB, HY_FILT_HID), HY_POS_EMB ** -0.5)
    inp['hy_f_b1'] = nrm((DEPTH, HY_FILT_HID), 0.1)
    inp['hy_f_w2'] = nrm((DEPTH, HY_FILT_HID, HY_FILT_HID), HY_FILT_HID ** -0.5)
    inp['hy_f_b2'] = nrm((DEPTH, HY_FILT_HID), 0.1)
    inp['hy_f_w3'] = nrm((DEPTH, HY_FILT_HID, 2 * HY_ORDER * HY_WIDTH), HY_FILT_HID ** -0.5)
    inp['hy_f_freq'] = 1.0 + nrm((DEPTH, HY_FILT_HID), 0.1)
    inp['hy_bias'] = nrm((DEPTH, HY_ORDER, HY_WIDTH), 1.0)
    inp['na_q_g'] = 1.0 + nrm((DEPTH, HEAD_DIM), 0.02)
    inp['na_k_g'] = 1.0 + nrm((DEPTH, HEAD_DIM), 0.02)
    inp['na_rpb'] = nrm((DEPTH, NA_HEADS, 2 * NA_WIN_H - 1, 2 * NA_WIN_W - 1), 0.02)
    inp['gqa_q_g'] = 1.0 + nrm((DEPTH, HEAD_DIM), 0.02)
    inp['gqa_k_g'] = 1.0 + nrm((DEPTH, HEAD_DIM), 0.02)
    inp['router_w'] = nrm((DEPTH, D, N_EXPERTS), D ** -0.5)
    inp['exp_w_gate'] = nrm((DEPTH, N_EXPERTS, D, EXPERT_FF), D ** -0.5)
    inp['exp_w_up'] = nrm((DEPTH, N_EXPERTS, D, EXPERT_FF), D ** -0.5)
    inp['exp_w_down'] = nrm((DEPTH, N_EXPERTS, EXPERT_FF, D), EXPERT_FF ** -0.5)
    return inp


def reference(x_prompt, x_sample, cache_na_k, cache_na_v, cache_gqa_k, cache_gqa_v, c, c_ctx,
              ada_w, ada_b, norm1_g, norm2_g, w_in, w_out, hy_short_w, hy_short_b,
              hy_f_w1, hy_f_b1, hy_f_w2, hy_f_b2, hy_f_w3, hy_f_freq, hy_bias,
              na_q_g, na_k_g, na_rpb, gqa_q_g, gqa_k_g,
              router_w, exp_w_gate, exp_w_up, exp_w_down):
    yp = x_prompt
    ys = x_sample
    na_ks, na_vs, g_ks, g_vs = [], [], [], []
    for l in range(DEPTH):
        lp = {
            'norm1_g': norm1_g[l], 'norm2_g': norm2_g[l], 'w_in': w_in[l], 'w_out': w_out[l],
            'hy_short_w': hy_short_w[l], 'hy_short_b': hy_short_b[l],
            'hy_f_w1': hy_f_w1[l], 'hy_f_b1': hy_f_b1[l], 'hy_f_w2': hy_f_w2[l], 'hy_f_b2': hy_f_b2[l],
            'hy_f_w3': hy_f_w3[l], 'hy_f_freq': hy_f_freq[l], 'hy_bias': hy_bias[l],
            'na_q_g': na_q_g[l], 'na_k_g': na_k_g[l], 'na_rpb': na_rpb[l],
            'gqa_q_g': gqa_q_g[l], 'gqa_k_g': gqa_k_g[l],
            'router_w': router_w[l], 'exp_w_gate': exp_w_gate[l], 'exp_w_up': exp_w_up[l],
            'exp_w_down': exp_w_down[l],
        }
        mod_ctx = (jax.nn.silu(c_ctx) @ ada_w[l] + ada_b[l])[None, :]
        yp, (nk, nv, gk, gv) = _layer(yp, mod_ctx, lp, None)
        na_ks.append(nk)
        na_vs.append(nv)
        g_ks.append(gk)
        g_vs.append(gv)
        mod_lat = jax.nn.silu(c) @ ada_w[l] + ada_b[l]
        ys, _ = _layer(ys, mod_lat, lp, (cache_na_k[:, l], cache_na_v[:, l], cache_gqa_k[:, l], cache_gqa_v[:, l]))
    new_na_k = jnp.stack(na_ks, axis=1)
    new_na_v = jnp.stack(na_vs, axis=1)
    new_gqa_k = jnp.stack(g_ks, axis=1)
    new_gqa_v = jnp.stack(g_vs, axis=1)
    return (yp, ys, new_na_k, new_na_v, new_gqa_k, new_gqa_v)
```

```python
import functools
import math

import numpy as np
import jax
import jax.numpy as jnp
from jax import lax
from jax.experimental import pallas as pl
from jax.experimental.pallas import tpu as pltpu

F32 = jnp.float32
BF16 = jnp.bfloat16

HEAD_DIM = 128
GRID_W = 64
HY_WIDTH = 512
HY_ORDER = 2
HY_IN = (HY_ORDER + 1) * HY_WIDTH
HY_POS_EMB = 33
HY_DECAY_TARGET = 1e-2
HY_FAST_PCT = 0.3
HY_SLOW_PCT = 1.5
NA_HEADS = 6
NA_WIN_H = 8
NA_WIN_W = 16
GQA_Q_HEADS = 6
GQA_KV_HEADS = 2
GQA_GROUP = GQA_Q_HEADS // GQA_KV_HEADS
ROPE_THETA = 10000.0
N_EXPERTS = 16
EC_CAPACITY_FACTOR = 2
NORM_EPS = 1e-6
MASK_VALUE = -1e30
NA_W = NA_HEADS * HEAD_DIM
GQ_W = GQA_Q_HEADS * HEAD_DIM
GKV_W = GQA_KV_HEADS * HEAD_DIM

LANES = 128
VMEM_LIMIT_BYTES = 56 * 1024 * 1024
DFT_INNER = 128
DIRECT_DFT_MAX_LEN = 512


def _cp(n_axes, vmem=VMEM_LIMIT_BYTES):
    return pltpu.CompilerParams(dimension_semantics=("arbitrary",) * n_axes, vmem_limit_bytes=vmem)


def _nt(a, b):
    return lax.dot_general(a, b, (((1,), (1,)), ((), ())), preferred_element_type=F32)


def _mm(a, b):
    return jnp.dot(a, b, preferred_element_type=F32)


def _mm_hi(a, b):
    return jnp.dot(a, b, preferred_element_type=F32, precision=lax.Precision.HIGHEST)


def _mod_kernel(c_ref, w_ref, b_ref, o_ref):
    c = c_ref[...]
    s = c * jax.nn.sigmoid(c)
    o_ref[0] = _mm(s.astype(BF16), w_ref[0].astype(BF16)) + b_ref[0]


def _modulation(cvec, ada_w, ada_b):
    depth, d, n = ada_w.shape
    rows = cvec.shape[0]
    tn = min(n, 512)
    return pl.pallas_call(
        _mod_kernel,
        out_shape=jax.ShapeDtypeStruct((depth, rows, n), F32),
        grid=(depth, n // tn),
        in_specs=[
            pl.BlockSpec((rows, d), lambda l, j: (0, 0)),
            pl.BlockSpec((1, d, tn), lambda l, j: (l, 0, j)),
            pl.BlockSpec((1, 1, tn), lambda l, j: (l, 0, j)),
        ],
        out_specs=pl.BlockSpec((1, rows, tn), lambda l, j: (l, 0, j)),
        compiler_params=_cp(2),
        name="modulation",
    )(cvec, ada_w, ada_b.reshape(depth, 1, n))


def _head_norm(t, g):
    return t * lax.rsqrt(jnp.mean(t * t, axis=-1, keepdims=True) + NORM_EPS) * g


def _rope(t, cos, sin):
    swapped = jnp.where((lax.broadcasted_iota(jnp.int32, t.shape, 1) % 64) < 32,
                        pltpu.roll(t, HEAD_DIM - 32, 1), pltpu.roll(t, 32, 1))
    return t * cos + swapped * sin


def _inproj_kernel(latent, x_ref, mod_ref, g1_ref, w_ref, hg_ref, *rest):
    if latent:
        cos_ref, sin_ref, hy_ref, qna_ref, kna_ref, vna_ref, qg_ref, kg_ref, vg_ref = rest
    else:
        hy_ref, qna_ref, qg_ref, nak_ref, nav_ref, gk_ref, gv_ref = rest
    d = x_ref.shape[-1]
    x = x_ref[...]
    mod = mod_ref[0]
    shift, scale = mod[:, 0:d], mod[:, d:2 * d]
    ms = jnp.mean(x * x, axis=-1, keepdims=True)
    h = (x * lax.rsqrt(ms + NORM_EPS) * g1_ref[...]) * (1.0 + scale) + shift
    hb = h.astype(BF16)

    def proj(c0, n):
        return _mm(hb, w_ref[:, c0:c0 + n])

    hy_ref[...] = proj(0, HY_IN)
    q_scale = HEAD_DIM ** -0.5
    g_naq, g_nak = hg_ref[0:1, :], hg_ref[1:2, :]
    g_gq, g_gk = hg_ref[2:3, :], hg_ref[3:4, :]
    c = HY_IN
    z_naq = proj(c, NA_W)
    z_nak = proj(c + NA_W, NA_W)
    z_nav = proj(c + 2 * NA_W, NA_W)
    c += 3 * NA_W
    z_gq = proj(c, GQ_W)
    z_gk = proj(c + GQ_W, GKV_W)
    z_gv = proj(c + GQ_W + GKV_W, GKV_W)
    if latent:
        cos, sin = cos_ref[...], sin_ref[...]
    seq = None if latent else nak_ref.shape[2]
    nb = None if latent else nak_ref.shape[0]
    for hd in range(NA_HEADS):
        sl = slice(hd * HEAD_DIM, (hd + 1) * HEAD_DIM)
        qna_ref[:, sl] = (_head_norm(z_naq[:, sl], g_naq) * q_scale).astype(BF16)
        k = _head_norm(z_nak[:, sl], g_nak)
        v = z_nav[:, sl]
        if latent:
            kna_ref[:, sl] = k.astype(BF16)
            vna_ref[:, sl] = v.astype(BF16)
        else:
            for b in range(nb):
                nak_ref[b, hd] = k[b * seq:(b + 1) * seq]
                nav_ref[b, hd] = v[b * seq:(b + 1) * seq]
    for hd in range(GQA_Q_HEADS):
        sl = slice(hd * HEAD_DIM, (hd + 1) * HEAD_DIM)
        q = _head_norm(z_gq[:, sl], g_gq)
        if latent:
            q = _rope(q, cos, sin)
        qg_ref[:, sl] = (q * q_scale).astype(BF16)
    for hd in range(GQA_KV_HEADS):
        sl = slice(hd * HEAD_DIM, (hd + 1) * HEAD_DIM)
        k = _head_norm(z_gk[:, sl], g_gk)
        v = z_gv[:, sl]
        if latent:
            kg_ref[:, sl] = _rope(k, cos, sin).astype(BF16)
            vg_ref[:, sl] = v.astype(BF16)
        else:
            for b in range(nb):
                gk_ref[b, hd] = k[b * seq:(b + 1) * seq]
                gv_ref[b, hd] = v[b * seq:(b + 1) * seq]


def _rope_tables(n):
    pos = np.arange(n)
    row = (pos // GRID_W).astype(np.float64)
    col = (pos % GRID_W).astype(np.float64)
    quarter = HEAD_DIM // 4
    inv = ROPE_THETA ** (-np.arange(quarter, dtype=np.float64) / quarter)
    ang_r = row[:, None] * inv[None, :]
    ang_c = col[:, None] * inv[None, :]
    cos = np.concatenate([np.cos(ang_r), np.cos(ang_r), np.cos(ang_c), np.cos(ang_c)], axis=-1)
    sin = np.concatenate([-np.sin(ang_r), np.sin(ang_r), -np.sin(ang_c), np.sin(ang_c)], axis=-1)
    return jnp.asarray(cos, F32), jnp.asarray(sin, F32)


def _in_projection(x2d, mod3, mod_base, batch, seq, g1, w_in_bf, head_gains, latent):
    t, d = x2d.shape
    tm = min(512, seq) if latent else min(512, t)
    if not latent:
        tm = max(tm, seq)
    nt_ = t // tm
    tiles_per_batch = seq // tm if latent else None
    wide = w_in_bf.shape[1]
    if latent:
        mod_idx = lambda i: (mod_base + i // tiles_per_batch, 0, 0)
    else:
        mod_idx = lambda i: (mod_base, 0, 0)
    in_specs = [
        pl.BlockSpec((tm, d), lambda i: (i, 0)),
        pl.BlockSpec((1, 1, mod3.shape[-1]), mod_idx),
        pl.BlockSpec((1, d), lambda i: (0, 0)),
        pl.BlockSpec((d, wide), lambda i: (0, 0), pipeline_mode=pl.Buffered(1)),
        pl.BlockSpec((4, HEAD_DIM), lambda i: (0, 0)),
    ]
    args = [x2d, mod3, g1.reshape(1, d), w_in_bf, head_gains]
    row_spec = lambda w: pl.BlockSpec((tm, w), lambda i: (i, 0))
    if latent:
        cos, sin = _rope_tables(seq)
        in_specs += [pl.BlockSpec((tm, HEAD_DIM), lambda i: (i % tiles_per_batch, 0))] * 2
        args += [cos, sin]
        out_shape = [
            jax.ShapeDtypeStruct((t, HY_IN), F32),
            jax.ShapeDtypeStruct((t, NA_W), BF16), jax.ShapeDtypeStruct((t, NA_W), BF16),
            jax.ShapeDtypeStruct((t, NA_W), BF16), jax.ShapeDtypeStruct((t, GQ_W), BF16),
            jax.ShapeDtypeStruct((t, GKV_W), BF16), jax.ShapeDtypeStruct((t, GKV_W), BF16),
        ]
        out_specs = [row_spec(HY_IN), row_spec(NA_W), row_spec(NA_W), row_spec(NA_W),
                     row_spec(GQ_W), row_spec(GKV_W), row_spec(GKV_W)]
    else:
        nb = tm // seq
        kv_spec = lambda hh: pl.BlockSpec((nb, hh, seq, HEAD_DIM), lambda i: (i, 0, 0, 0))
        out_shape = [
            jax.ShapeDtypeStruct((t, HY_IN), F32),
            jax.ShapeDtypeStruct((t, NA_W), BF16), jax.ShapeDtypeStruct((t, GQ_W), BF16),
            jax.ShapeDtypeStruct((batch, NA_HEADS, seq, HEAD_DIM), F32),
            jax.ShapeDtypeStruct((batch, NA_HEADS, seq, HEAD_DIM), F32),
            jax.ShapeDtypeStruct((batch, GQA_KV_HEADS, seq, HEAD_DIM), F32),
            jax.ShapeDtypeStruct((batch, GQA_KV_HEADS, seq, HEAD_DIM), F32),
        ]
        out_specs = [row_spec(HY_IN), row_spec(NA_W), row_spec(GQ_W),
                     kv_spec(NA_HEADS), kv_spec(NA_HEADS), kv_spec(GQA_KV_HEADS), kv_spec(GQA_KV_HEADS)]
    return pl.pallas_call(
        functools.partial(_inproj_kernel, latent),
        out_shape=out_shape,
        grid=(nt_,),
        in_specs=in_specs,
        out_specs=out_specs,
        compiler_params=_cp(1),
        name="in_proj_lat" if latent else "in_proj_ctx",
    )(*args)


def _shortconv_kernel(x_ref, w_ref, b_ref, o_ref):
    x = x_ref[0]
    n = x.shape[0]
    rows = lax.broadcasted_iota(jnp.int32, x.shape, 0)
    prev = jnp.where(rows == 0, 0.0, pltpu.roll(x, 1, 0))
    nxt = jnp.where(rows == n - 1, 0.0, pltpu.roll(x, n - 1, 0))
    o_ref[0, 0] = prev * w_ref[0:1, :] + x * w_ref[1:2, :] + nxt * w_ref[2:3, :] + b_ref[...]


def _short_conv(hy3, sw, sb):
    b, l, _ = hy3.shape
    tc = 256
    per = HY_WIDTH // tc
    return pl.pallas_call(
        _shortconv_kernel,
        out_shape=jax.ShapeDtypeStruct((HY_ORDER + 1, b, l, HY_WIDTH), F32),
        grid=(b, HY_IN // tc),
        in_specs=[
            pl.BlockSpec((1, l, tc), lambda i, q: (i, 0, q)),
            pl.BlockSpec((3, tc), lambda i, q: (0, q)),
            pl.BlockSpec((1, tc), lambda i, q: (0, q)),
        ],
        out_specs=pl.BlockSpec((1, 1, l, tc), lambda i, q: (q // per, i, 0, q % per)),
        compiler_params=_cp(2),
        name="hy_short_conv",
    )(hy3, sw, sb.reshape(1, HY_IN))


def _filter_kernel(z_ref, w1_ref, b1_ref, w2_ref, b2_ref, w3_ref, fr_ref, dl_ref, f_ref, n_ref):
    i = pl.program_id(0)
    z = z_ref[...]
    sf = fr_ref[...]
    hid = jnp.sin(sf * (_mm_hi(z, w1_ref[...]) + b1_ref[...]))
    hid = jnp.sin(sf * (_mm_hi(hid, w2_ref[...]) + b2_ref[...]))
    filt = _mm_hi(hid, w3_ref[...])
    t = z[:, 0:1]
    decay = jnp.exp(-t * dl_ref[...])
    decay = jnp.concatenate([decay] * (2 * HY_ORDER), axis=-1)
    filt = filt * decay
    half = HY_ORDER * HY_WIDTH
    rows = lax.broadcasted_iota(jnp.int32, filt.shape, 0) + i * filt.shape[0]
    cols = lax.broadcasted_iota(jnp.int32, filt.shape, 1)
    filt = jnp.where((rows == 0) & (cols >= half), 0.0, filt)
    f_ref[...] = filt
    part = jnp.sum(jnp.abs(filt), axis=0, keepdims=True)

    @pl.when(i == 0)
    def _():
        n_ref[...] = jnp.zeros_like(n_ref)

    n_ref[...] += part[:, :half] + part[:, half:]


def _filter_embedding(l):
    t = np.linspace(0.0, 1.0, l, dtype=np.float64)[:, None]
    bands = (HY_POS_EMB - 1) // 2
    fr = np.linspace(1e-4, bands - 1, bands, dtype=np.float64)[None, :]
    w = 2.0 * math.pi * np.arange(l, dtype=np.float64)[:, None] / l
    z = np.concatenate([t, np.cos(fr * w), -np.sin(fr * w)], axis=-1)
    zp = np.zeros((l, LANES), np.float32)
    zp[:, :HY_POS_EMB] = z
    min_d = abs(math.log(HY_DECAY_TARGET) / HY_SLOW_PCT)
    max_d = abs(math.log(HY_DECAY_TARGET) / HY_FAST_PCT)
    deltas = np.linspace(min_d, max_d, HY_WIDTH, dtype=np.float64)[None, :]
    return jnp.asarray(zp, F32), jnp.asarray(deltas, F32)


def _hyena_filters(l, w1, b1, w2, b2, w3, freq):
    zp, deltas = _filter_embedding(l)
    hid = w1.shape[1]
    w1p = jnp.zeros((LANES, hid), F32).at[:HY_POS_EMB].set(w1)
    tl = min(l, 512)
    wide = w3.shape[1]
    full = lambda a: pl.BlockSpec(a.shape, lambda i: (0,) * a.ndim)
    b1r, b2r, frr = b1.reshape(1, hid), b2.reshape(1, hid), freq.reshape(1, hid)
    return pl.pallas_call(
        _filter_kernel,
        out_shape=[jax.ShapeDtypeStruct((l, wide), F32),
                   jax.ShapeDtypeStruct((1, HY_ORDER * HY_WIDTH), F32)],
        grid=(l // tl,),
        in_specs=[pl.BlockSpec((tl, LANES), lambda i: (i, 0)), full(w1p), full(b1r), full(w2), full(b2r),
                  full(w3), full(frr), full(deltas)],
        out_specs=[pl.BlockSpec((tl, wide), lambda i: (i, 0)),
                   pl.BlockSpec((1, HY_ORDER * HY_WIDTH), lambda i: (0, 0))],
        compiler_params=_cp(1),
        name="hy_filters",
    )(zp, w1p, b1r, w2, b2r, w3, frr, deltas)


def _bf16_const(a):
    return jnp.asarray(a, F32).astype(BF16)


def _dft_plan(l):
    n = 2 * l
    n2_len = 1 if l <= DIRECT_DFT_MAX_LEN else DFT_INNER
    n1_len = n // n2_len
    k1 = np.arange(n1_len, dtype=np.float64)[:, None]
    n1 = np.arange(n1_len // 2, dtype=np.float64)[None, :]
    ang = 2.0 * np.pi * k1 * n1 / n1_len
    plan = {
        "n1": n1_len, "n2": n2_len,
        "fs": _bf16_const(np.concatenate([np.cos(ang), -np.sin(ang)], axis=0)),
        "gr": _bf16_const(np.cos(ang).T / n),
        "gi": _bf16_const(-np.sin(ang).T / n),
    }
    if n2_len > 1:
        k2 = np.arange(n2_len, dtype=np.float64)
        a2 = 2.0 * np.pi * np.outer(k2, k2) / n2_len
        fr, fi = np.cos(a2), -np.sin(a2)
        plan["d2"] = _bf16_const(np.block([[fr, -fi], [fi, fr]]))
        plan["d2c"] = _bf16_const(np.block([[fr, fi], [-fi, fr]]))
        at = 2.0 * np.pi * np.arange(n1_len, dtype=np.float64)[:, None] * k2[None, :] / n
        plan["twr"] = jnp.asarray(np.repeat(np.cos(at)[:, :, None], LANES, axis=2), F32)
        plan["twi"] = jnp.asarray(np.repeat(-np.sin(at)[:, :, None], LANES, axis=2), F32)
    return plan


def _dft1_kernel(z_ref, fs_ref, ar_ref, ai_ref):
    a = _mm(fs_ref[...], z_ref[0, 0].astype(BF16))
    n1 = ar_ref.shape[1]
    ar_ref[0] = a[:n1].astype(BF16)
    ai_ref[0] = a[n1:].astype(BF16)


def _dft_stage1(z4, part, plan):
    _, b, kk, nc = z4.shape
    n1 = plan["n1"]
    tn = min(nc, 4096)
    return pl.pallas_call(
        _dft1_kernel,
        out_shape=[jax.ShapeDtypeStruct((b, n1, nc), BF16)] * 2,
        grid=(b, nc // tn),
        in_specs=[pl.BlockSpec((1, 1, kk, tn), lambda i, j: (part, i, 0, j)),
                  pl.BlockSpec((2 * n1, kk), lambda i, j: (0, 0))],
        out_specs=[pl.BlockSpec((1, n1, tn), lambda i, j: (i, 0, j))] * 2,
        compiler_params=_cp(2),
        name="hy_dft_outer",
    )(z4, plan["fs"])


def _twiddle_inner_dft(ar, ai, twr, twi, d2):
    c = ar.shape[-1]
    twr = jnp.concatenate([twr] * (c // LANES), axis=-1)
    twi = jnp.concatenate([twi] * (c // LANES), axis=-1)
    zr = ar * twr - ai * twi
    zi = ar * twi + ai * twr
    x = _mm(d2, jnp.concatenate([zr, zi], axis=0).astype(BF16))
    n2 = ar.shape[0]
    return x[:n2], x[n2:], twr, twi


def _spec_filter_kernel(two_stage, kb, ar_ref, ai_ref, nrm_ref, *rest):
    if two_stage:
        twr_ref, twi_ref, d2_ref, kr_ref, ki_ref = rest
    else:
        kr_ref, ki_ref = rest
    half = HY_ORDER * HY_WIDTH
    inv = 1.0 / nrm_ref[...]

    def combine(xr, xi):
        return (xr[:, :half] + xr[:, half:]) * inv, (xi[:, :half] - xi[:, half:]) * inv

    if two_stage:
        for j in range(kb):
            xr, xi, _, _ = _twiddle_inner_dft(ar_ref[0, j].astype(F32), ai_ref[0, j].astype(F32),
                                              twr_ref[j], twi_ref[j], d2_ref[...])
            kr, ki = combine(xr, xi)
            for o in range(HY_ORDER):
                kr_ref[o, j] = kr[:, o * HY_WIDTH:(o + 1) * HY_WIDTH]
                ki_ref[o, j] = ki[:, o * HY_WIDTH:(o + 1) * HY_WIDTH]
    else:
        kr, ki = combine(ar_ref[0].astype(F32), ai_ref[0].astype(F32))
        for o in range(HY_ORDER):
            kr_ref[o] = kr[:, o * HY_WIDTH:(o + 1) * HY_WIDTH]
            ki_ref[o] = ki[:, o * HY_WIDTH:(o + 1) * HY_WIDTH]


def _filter_spectrum(filt, norm, plan):
    l, wide = filt.shape
    n1, n2 = plan["n1"], plan["n2"]
    ar, ai = _dft_stage1(filt.reshape(1, 1, n1 // 2, n2 * wide), 0, plan)
    two_stage = n2 > 1
    if two_stage:
        kb = 8
        ar = ar.reshape(1, n1, n2, wide)
        ai = ai.reshape(1, n1, n2, wide)
        a_spec = pl.BlockSpec((1, kb, n2, wide), lambda i: (0, i, 0, 0))
        in_specs = [a_spec, a_spec, pl.BlockSpec(norm.shape, lambda i: (0, 0)),
                    pl.BlockSpec((kb, n2, LANES), lambda i: (i, 0, 0)),
                    pl.BlockSpec((kb, n2, LANES), lambda i: (i, 0, 0)),
                    pl.BlockSpec((2 * n2, 2 * n2), lambda i: (0, 0))]
        args = [ar, ai, norm, plan["twr"], plan["twi"], plan["d2"]]
        out_shape = [jax.ShapeDtypeStruct((HY_ORDER, n1, n2, HY_WIDTH), F32)] * 2
        out_specs = [pl.BlockSpec((HY_ORDER, kb, n2, HY_WIDTH), lambda i: (0, i, 0, 0))] * 2
    else:
        kb = min(n1, 256)
        a_spec = pl.BlockSpec((1, kb, wide), lambda i: (0, i, 0))
        in_specs = [a_spec, a_spec, pl.BlockSpec(norm.shape, lambda i: (0, 0))]
        args = [ar, ai, norm]
        out_shape = [jax.ShapeDtypeStruct((HY_ORDER, n1, HY_WIDTH), F32)] * 2
        out_specs = [pl.BlockSpec((HY_ORDER, kb, HY_WIDTH), lambda i: (0, i, 0))] * 2
    return pl.pallas_call(
        functools.partial(_spec_filter_kernel, two_stage, kb),
        out_shape=out_shape,
        grid=(n1 // kb,),
        in_specs=in_specs,
        out_specs=out_specs,
        compiler_params=_cp(1),
        name="hy_filter_spectrum",
    )(*args)


def _spec_mul_kernel(two_stage, kb, ar_ref, ai_ref, kr_ref, ki_ref, *rest):
    if two_stage:
        twr_ref, twi_ref, d2_ref, d2c_ref, br_ref, bi_ref = rest
        for j in range(kb):
            xr, xi, twr, twi = _twiddle_inner_dft(ar_ref[0, j].astype(F32), ai_ref[0, j].astype(F32),
                                                  twr_ref[j], twi_ref[j], d2_ref[...])
            kr, ki = kr_ref[0, j], ki_ref[0, j]
            yr = xr * kr - xi * ki
            yi = xr * ki + xi * kr
            bm = _mm(d2c_ref[...], jnp.concatenate([yr, yi], axis=0).astype(BF16))
            n2 = xr.shape[0]
            br, bi = bm[:n2], bm[n2:]
            br_ref[0, j] = (br * twr + bi * twi).astype(BF16)
            bi_ref[0, j] = (bi * twr - br * twi).astype(BF16)
    else:
        br_ref, bi_ref = rest
        xr, xi = ar_ref[0].astype(F32), ai_ref[0].astype(F32)
        kr, ki = kr_ref[0], ki_ref[0]
        br_ref[0] = (xr * kr - xi * ki).astype(BF16)
        bi_ref[0] = (xr * ki + xi * kr).astype(BF16)


def _spectrum_multiply(ar, ai, kr, ki, order, plan):
    b, n1, nc = ar.shape
    n2 = plan["n2"]
    c = nc // n2
    two_stage = n2 > 1
    if two_stage:
        kb = 8
        ar = ar.reshape(b, n1, n2, c)
        ai = ai.reshape(b, n1, n2, c)
        a_spec = pl.BlockSpec((1, kb, n2, c), lambda i, j: (i, j, 0, 0))
        k_spec = pl.BlockSpec((1, kb, n2, c), lambda i, j: (order, j, 0, 0))
        tw_spec = pl.BlockSpec((kb, n2, LANES), lambda i, j: (j, 0, 0))
        d_spec = pl.BlockSpec((2 * n2, 2 * n2), lambda i, j: (0, 0))
        in_specs = [a_spec, a_spec, k_spec, k_spec, tw_spec, tw_spec, d_spec, d_spec]
        args = [ar, ai, kr, ki, plan["twr"], plan["twi"], plan["d2"], plan["d2c"]]
        out_shape = [jax.ShapeDtypeStruct((b, n1, n2, c), BF16)] * 2
        out_specs = [a_spec, a_spec]
    else:
        kb = min(n1, 256)
        a_spec = pl.BlockSpec((1, kb, c), lambda i, j: (i, j, 0))
        k_spec = pl.BlockSpec((1, kb, c), lambda i, j: (order, j, 0))
        in_specs = [a_spec, a_spec, k_spec, k_spec]
        args = [ar, ai, kr, ki]
        out_shape = [jax.ShapeDtypeStruct((b, n1, c), BF16)] * 2
        out_specs = [a_spec, a_spec]
    br, bi = pl.pallas_call(
        functools.partial(_spec_mul_kernel, two_stage, kb),
        out_shape=out_shape,
        grid=(b, n1 // kb),
        in_specs=in_specs,
        out_specs=out_specs,
        compiler_params=_cp(2),
        name="hy_spectrum_mul",
    )(*args)
    return br.reshape(b, n1, nc), bi.reshape(b, n1, nc)


def _idft1_kernel(br_ref, bi_ref, gr_ref, gi_ref, z_ref, x_ref, bias_ref, o_ref):
    conv = _mm(gr_ref[...], br_ref[0]) + _mm(gi_ref[...], bi_ref[0])
    z = z_ref[0, 0]
    o_ref[0, 0] = (x_ref[0, 0] * (conv + z * bias_ref[...])).astype(o_ref.dtype)


def _idft_gate(br, bi, plan, z4, z_part, u4, x_part, bias_row, out_dtype):
    b, n1, nc = br.shape
    kk = n1 // 2
    tn = min(nc, 4096)
    return pl.pallas_call(
        _idft1_kernel,
        out_shape=jax.ShapeDtypeStruct((1, b, kk, nc), out_dtype),
        grid=(b, nc // tn),
        in_specs=[pl.BlockSpec((1, n1, tn), lambda i, j: (i, 0, j)),
                  pl.BlockSpec((1, n1, tn), lambda i, j: (i, 0, j)),
                  pl.BlockSpec((kk, n1), lambda i, j: (0, 0)),
                  pl.BlockSpec((kk, n1), lambda i, j: (0, 0)),
                  pl.BlockSpec((1, 1, kk, tn), lambda i, j: (z_part, i, 0, j)),
                  pl.BlockSpec((1, 1, kk, tn), lambda i, j: (x_part, i, 0, j)),
                  pl.BlockSpec((1, tn), lambda i, j: (0, j))],
        out_specs=pl.BlockSpec((1, 1, kk, tn), lambda i, j: (0, i, 0, j)),
        compiler_params=_cp(2),
        name="hy_idft_gate",
    )(br, bi, plan["gr"], plan["gi"], z4, u4, bias_row)


def _hyena(hy3, lp):
    b, l, _ = hy3.shape
    plan = _dft_plan(l)
    n1, n2 = plan["n1"], plan["n2"]
    filt, norm = _hyena_filters(l, lp["hy_f_w1"], lp["hy_f_b1"], lp["hy_f_w2"], lp["hy_f_b2"],
                                lp["hy_f_w3"], lp["hy_f_freq"])
    kr, ki = _filter_spectrum(filt, norm, plan)
    u = _short_conv(hy3, lp["hy_short_w"], lp["hy_short_b"])
    u4 = u.reshape(HY_ORDER + 1, b, n1 // 2, n2 * HY_WIDTH)
    z4, z_part = u4, 0
    for o in range(HY_ORDER):
        ar, ai = _dft_stage1(z4, z_part, plan)
        br, bi = _spectrum_multiply(ar, ai, kr, ki, o, plan)
        bias_row = jnp.tile(lp["hy_bias"][o].reshape(1, HY_WIDTH), (1, n2))
        last = o == HY_ORDER - 1
        z4 = _idft_gate(br, bi, plan, z4, z_part, u4, o + 1, bias_row, BF16 if last else F32)
        z_part = 0
    return z4.reshape(b * l, HY_WIDTH)


def _softmax_pv(s_list, v_list):
    m = s_list[0].max(axis=-1, keepdims=True)
    for s in s_list[1:]:
        m = jnp.maximum(m, s.max(axis=-1, keepdims=True))
    den = 0.0
    o = 0.0
    for s, v in zip(s_list, v_list):
        e = jnp.exp(s - m)
        den = den + e.sum(axis=-1, keepdims=True)
        o = o + _mm(e.astype(BF16), v)
    return o / den


def _ctx_attn_kernel(qna_ref, qg_ref, nak_ref, nav_ref, gk_ref, gv_ref, ona_ref, og_ref):
    for hd in range(NA_HEADS):
        sl = slice(hd * HEAD_DIM, (hd + 1) * HEAD_DIM)
        k = nak_ref[0, hd].astype(BF16)
        v = nav_ref[0, hd].astype(BF16)
        ona_ref[:, sl] = _softmax_pv([_nt(qna_ref[:, sl], k)], [v]).astype(BF16)
    for g in range(GQA_KV_HEADS):
        k = gk_ref[0, g].astype(BF16)
        v = gv_ref[0, g].astype(BF16)
        for r in range(GQA_GROUP):
            hd = g * GQA_GROUP + r
            sl = slice(hd * HEAD_DIM, (hd + 1) * HEAD_DIM)
            og_ref[:, sl] = _softmax_pv([_nt(qg_ref[:, sl], k)], [v]).astype(BF16)


def _ctx_attention(qna, qg, nak, nav, gk, gv):
    batch, _, seq, _ = nak.shape
    t = batch * seq
    kv = lambda hh: pl.BlockSpec((1, hh, seq, HEAD_DIM), lambda i: (i, 0, 0, 0))
    row = lambda w: pl.BlockSpec((seq, w), lambda i: (i, 0))
    return pl.pallas_call(
        _ctx_attn_kernel,
        out_shape=[jax.ShapeDtypeStruct((t, NA_W), BF16), jax.ShapeDtypeStruct((t, GQ_W), BF16)],
        grid=(batch,),
        in_specs=[row(NA_W), row(GQ_W), kv(NA_HEADS), kv(NA_HEADS), kv(GQA_KV_HEADS), kv(GQA_KV_HEADS)],
        out_specs=[row(NA_W), row(GQ_W)],
        compiler_params=_cp(1),
        name="ctx_attention",
    )(qna, qg, nak, nav, gk, gv)


NA_Q_ROWS = 2


def _na_tables(rows):
    r_q = NA_Q_ROWS
    kh = min(NA_WIN_H, rows)
    win = min(r_q + kh, rows)
    nblk = rows // r_q
    starts, var_ids, variants, keys = [], [], [], {}
    qr = np.repeat(np.arange(r_q), GRID_W)[:, None]
    qc = np.tile(np.arange(GRID_W), r_q)[:, None]
    kr = np.repeat(np.arange(win), GRID_W)[None, :]
    kc = np.tile(np.arange(GRID_W), win)[None, :]
    cstart = np.clip(qc - NA_WIN_W // 2, 0, GRID_W - NA_WIN_W)
    col_ok = (kc >= cstart) & (kc < cstart + NA_WIN_W)
    cidx = np.clip(kc - qc + NA_WIN_W - 1, 0, 2 * NA_WIN_W - 2)
    for j in range(nblk):
        start = int(np.clip(r_q * j - kh // 2, 0, rows - win))
        r = r_q * j + qr
        rs = np.clip(r - kh // 2, 0, rows - kh)
        kabs = start + kr
        row_ok = (kabs >= rs) & (kabs < rs + kh)
        ridx = np.clip(kabs - r + NA_WIN_H - 1, 0, 2 * NA_WIN_H - 2)
        ok = row_ok & col_ok
        key = (ridx * ok).tobytes() + ok.tobytes()
        if key not in keys:
            keys[key] = len(variants)
            variants.append((np.broadcast_to(ridx, ok.shape), np.broadcast_to(cidx, ok.shape), ok))
        starts.append(start)
        var_ids.append(keys[key])
    ridx = np.stack([v[0] for v in variants])
    cidx = np.stack([v[1] for v in variants])
    ok = np.stack([v[2] for v in variants])
    return win, np.asarray(starts, np.int32), np.asarray(var_ids, np.int32), ridx, cidx, ok


def _na_lat_kernel(win, start_ref, var_ref, q_ref, k_ref, v_ref, kc_ref, vc_ref, bias_ref, o_ref):
    j = pl.program_id(2)
    off = pl.multiple_of(start_ref[j] * GRID_W, GRID_W)
    q = q_ref[...]
    kw = k_ref[pl.ds(off, win * GRID_W), :]
    vw = v_ref[pl.ds(off, win * GRID_W), :]
    s_loc = _nt(q, kw) + bias_ref[0, 0]
    s_ctx = _nt(q, kc_ref[0, 0, 0].astype(BF16))
    o_ref[...] = _softmax_pv([s_loc, s_ctx], [vw, vc_ref[0, 0, 0].astype(BF16)]).astype(BF16)


def _na_latent(q, k, v, kc, vc, layer, rpb, batch, n):
    rows = n // GRID_W
    win, starts, var_ids, ridx, cidx, ok = _na_tables(rows)
    bias = jnp.where(jnp.asarray(ok)[None], rpb[:, ridx, cidx], MASK_VALUE).astype(F32)
    r_q = NA_Q_ROWS
    nblk = rows // r_q
    tq = r_q * GRID_W
    wk = win * GRID_W
    past = kc.shape[3]
    grid_spec = pltpu.PrefetchScalarGridSpec(
        num_scalar_prefetch=2,
        grid=(batch, NA_HEADS, nblk),
        in_specs=[
            pl.BlockSpec((tq, HEAD_DIM), lambda b, h, j, st, vr: (b * nblk + j, h)),
            pl.BlockSpec((n, HEAD_DIM), lambda b, h, j, st, vr: (b, h)),
            pl.BlockSpec((n, HEAD_DIM), lambda b, h, j, st, vr: (b, h)),
            pl.BlockSpec((1, 1, 1, past, HEAD_DIM), lambda b, h, j, st, vr: (b, layer, h, 0, 0)),
            pl.BlockSpec((1, 1, 1, past, HEAD_DIM), lambda b, h, j, st, vr: (b, layer, h, 0, 0)),
            pl.BlockSpec((1, 1, tq, wk), lambda b, h, j, st, vr: (h, vr[j], 0, 0)),
        ],
        out_specs=pl.BlockSpec((tq, HEAD_DIM), lambda b, h, j, st, vr: (b * nblk + j, h)),
    )
    return pl.pallas_call(
        functools.partial(_na_lat_kernel, win),
        out_shape=jax.ShapeDtypeStruct((batch * n, NA_W), BF16),
        grid_spec=grid_spec,
        compiler_params=_cp(3),
        name="na_latent",
    )(jnp.asarray(starts), jnp.asarray(var_ids), q, k, v, kc, vc, bias)


GQA_KEY_CHUNK = 512


def _gqa_lat_kernel(q_ref, k_ref, v_ref, kc_ref, vc_ref, o_ref, m_ref, l_ref, acc_ref):
    tq = q_ref.shape[0]
    q = jnp.concatenate([q_ref[:, r * HEAD_DIM:(r + 1) * HEAD_DIM] for r in range(GQA_GROUP)], axis=0)
    m_ref[...] = jnp.full(m_ref.shape, -jnp.inf, F32)
    l_ref[...] = jnp.zeros(l_ref.shape, F32)
    acc_ref[...] = jnp.zeros(acc_ref.shape, F32)

    def update(kk, vv):
        s = _nt(q, kk)
        m_old = m_ref[...]
        m_new = jnp.maximum(m_old, s.max(axis=-1, keepdims=True))
        alpha = jnp.exp(m_old - m_new)
        e = jnp.exp(s - m_new)
        l_ref[...] = alpha * l_ref[...] + e.sum(axis=-1, keepdims=True)
        acc_ref[...] = alpha * acc_ref[...] + _mm(e.astype(BF16), vv)
        m_ref[...] = m_new

    n = k_ref.shape[0]
    tk = min(GQA_KEY_CHUNK, n)

    def body(c, carry):
        off = pl.multiple_of(c * tk, tk)
        update(k_ref[pl.ds(off, tk), :], v_ref[pl.ds(off, tk), :])
        return carry

    lax.fori_loop(0, n // tk, body, 0)
    update(kc_ref[0, 0, 0].astype(BF16), vc_ref[0, 0, 0].astype(BF16))
    o = acc_ref[...] / l_ref[...]
    for r in range(GQA_GROUP):
        o_ref[:, r * HEAD_DIM:(r + 1) * HEAD_DIM] = o[r * tq:(r + 1) * tq].astype(BF16)


def _gqa_latent(q, k, v, kc, vc, layer, batch, n):
    tq = min(256, n)
    nq = n // tq
    past = kc.shape[3]
    gw = GQA_GROUP * HEAD_DIM
    return pl.pallas_call(
        _gqa_lat_kernel,
        out_shape=jax.ShapeDtypeStruct((batch * n, GQ_W), BF16),
        grid=(batch, GQA_KV_HEADS, nq),
        in_specs=[
            pl.BlockSpec((tq, gw), lambda b, g, i: (b * nq + i, g)),
            pl.BlockSpec((n, HEAD_DIM), lambda b, g, i: (b, g)),
            pl.BlockSpec((n, HEAD_DIM), lambda b, g, i: (b, g)),
            pl.BlockSpec((1, 1, 1, past, HEAD_DIM), lambda b, g, i: (b, layer, g, 0, 0)),
            pl.BlockSpec((1, 1, 1, past, HEAD_DIM), lambda b, g, i: (b, layer, g, 0, 0)),
        ],
        out_specs=pl.BlockSpec((tq, gw), lambda b, g, i: (b * nq + i, g)),
        scratch_shapes=[pltpu.VMEM((GQA_GROUP * tq, 1), F32), pltpu.VMEM((GQA_GROUP * tq, 1), F32),
                        pltpu.VMEM((GQA_GROUP * tq, HEAD_DIM), F32)],
        compiler_params=_cp(3),
        name="gqa_latent",
    )(q, k, v, kc, vc)


def _outproj_kernel(x_ref, hy_ref, ona_ref, og_ref, mod_ref, g2_ref, wo_ref, rwh_ref, rwl_ref,
                    x1_ref, h2_ref, aff_ref):
    d = x_ref.shape[-1]
    mod = mod_ref[0]
    gate1 = mod[:, 2 * d:3 * d]
    shift2, scale2 = mod[:, 3 * d:4 * d], mod[:, 4 * d:5 * d]
    c1 = HY_WIDTH
    c2 = HY_WIDTH + NA_W
    c3 = c2 + GQ_W
    mixed = (_mm(hy_ref[...], wo_ref[0:c1, :]) + _mm(ona_ref[...], wo_ref[c1:c2, :])
             + _mm(og_ref[...], wo_ref[c2:c3, :]))
    x1 = x_ref[...] + gate1 * mixed
    x1_ref[...] = x1
    ms = jnp.mean(x1 * x1, axis=-1, keepdims=True)
    h2 = (x1 * lax.rsqrt(ms + NORM_EPS) * g2_ref[...]) * (1.0 + scale2) + shift2
    h2h = h2.astype(BF16)
    h2_ref[...] = h2h
    h2l = (h2 - h2h.astype(F32)).astype(BF16)
    logits = _nt(rwh_ref[...], h2h) + _nt(rwh_ref[...], h2l) + _nt(rwl_ref[...], h2h)
    m = logits.max(axis=0, keepdims=True)
    e = jnp.exp(logits - m)
    aff_ref[...] = e / e.sum(axis=0, keepdims=True)


def _out_projection(x2d, hy, ona, og, mod3, mod_base, seq, g2, w_out_bf, rw_hi, rw_lo, latent):
    t, d = x2d.shape
    tm = min(512, seq) if latent else min(512, t)
    tiles_per_batch = max(seq // tm, 1)
    if latent:
        mod_idx = lambda i: (mod_base + i // tiles_per_batch, 0, 0)
    else:
        mod_idx = lambda i: (mod_base, 0, 0)
    row = lambda w: pl.BlockSpec((tm, w), lambda i: (i, 0))
    ne = rw_hi.shape[0]
    return pl.pallas_call(
        _outproj_kernel,
        out_shape=[jax.ShapeDtypeStruct((t, d), F32), jax.ShapeDtypeStruct((t, d), BF16),
                   jax.ShapeDtypeStruct((ne, t), F32)],
        grid=(t // tm,),
        in_specs=[row(d), row(HY_WIDTH), row(NA_W), row(GQ_W),
                  pl.BlockSpec((1, 1, mod3.shape[-1]), mod_idx),
                  pl.BlockSpec((1, d), lambda i: (0, 0)),
                  pl.BlockSpec(w_out_bf.shape, lambda i: (0, 0), pipeline_mode=pl.Buffered(1)),
                  pl.BlockSpec((ne, d), lambda i: (0, 0)),
                  pl.BlockSpec((ne, d), lambda i: (0, 0))],
        out_specs=[row(d), row(d), pl.BlockSpec((ne, tm), lambda i: (0, i))],
        compiler_params=_cp(1),
        name="out_proj_lat" if latent else "out_proj_ctx",
    )(x2d, hy, ona, og, mod3, g2.reshape(1, d), w_out_bf, rw_hi, rw_lo)


ROUTE_CHUNK = 256


def _prefix_count(mask_f, tri):
    ne, n = mask_f.shape
    ch = tri.shape[0]
    pieces = []
    carry = jnp.zeros((ne, 1), F32)
    for c in range(n // ch):
        mk = mask_f[:, c * ch:(c + 1) * ch]
        inc = _mm(mk.astype(BF16), tri)
        pieces.append(inc - mk + carry)
        carry = carry + inc[:, ch - 1:ch]
    return (pieces[0] if len(pieces) == 1 else jnp.concatenate(pieces, axis=1)), carry


def _route_kernel(cap, aff_ref, slot_ref, slott_ref, gslot_ref):
    aff = aff_ref[...]
    ne, n = aff.shape

    def bisect(i, thr_bits):
        cand = thr_bits | (jnp.int32(1) << (30 - i))
        cnt = jnp.sum(jnp.where(aff >= pltpu.bitcast(cand, F32), 1.0, 0.0), axis=1, keepdims=True)
        return jnp.where(cnt >= cap, cand, thr_bits)

    thr = pltpu.bitcast(lax.fori_loop(0, 31, bisect, jnp.zeros((ne, 1), jnp.int32)), F32)
    ch = min(ROUTE_CHUNK, n)
    tri = jnp.where(lax.broadcasted_iota(jnp.int32, (ch, ch), 0) <= lax.broadcasted_iota(jnp.int32, (ch, ch), 1),
                    1.0, 0.0).astype(BF16)
    gt = jnp.where(aff > thr, 1.0, 0.0)
    eq = jnp.where(aff == thr, 1.0, 0.0)
    n_gt = jnp.sum(gt, axis=1, keepdims=True)
    eq_rank, _ = _prefix_count(eq, tri)
    sel = gt + eq * (eq_rank < (cap - n_gt)).astype(F32)
    rank, _ = _prefix_count(sel, tri)
    slot = jnp.where(sel > 0.0, rank, -1.0).astype(jnp.int32)
    slot_ref[...] = slot
    pad = jnp.zeros((LANES - ne, ch), F32)
    for c in range(n // ch):
        blk = jnp.concatenate([slot[:, c * ch:(c + 1) * ch].astype(F32), pad], axis=0)
        slott_ref[c * ch:(c + 1) * ch, :] = blk.T[:, :ne].astype(jnp.int32)
    srow = lax.broadcasted_iota(jnp.int32, (cap, n), 0)
    for e in range(ne):
        hit = slot[e:e + 1, :] == srow
        gslot_ref[0, e] = jnp.sum(jnp.where(hit, aff[e:e + 1, :], 0.0), axis=1, keepdims=True)


def _route(aff_t, n_sets, n):
    ne, t = aff_t.shape
    cap = EC_CAPACITY_FACTOR * n // N_EXPERTS
    return pl.pallas_call(
        functools.partial(_route_kernel, cap),
        out_shape=[jax.ShapeDtypeStruct((ne, t), jnp.int32), jax.ShapeDtypeStruct((t, ne), jnp.int32),
                   jax.ShapeDtypeStruct((n_sets, ne, cap, 1), F32)],
        grid=(n_sets,),
        in_specs=[pl.BlockSpec((ne, n), lambda b: (0, b))],
        out_specs=[pl.BlockSpec((ne, n), lambda b: (0, b)), pl.BlockSpec((n, ne), lambda b: (b, 0)),
                   pl.BlockSpec((1, ne, cap, 1), lambda b: (b, 0, 0, 0))],
        compiler_params=_cp(1),
        name="route",
    )(aff_t)


GATHER_ROWS = 512


def _gather_kernel(slot_ref, h_ref, o_ref):
    eb, _, cap, _ = o_ref.shape
    n = slot_ref.shape[-1]
    srow = lax.broadcasted_iota(jnp.int32, (cap, n), 0)
    onehot = [jnp.where(slot_ref[e] == srow, 1.0, 0.0).astype(BF16) for e in range(eb)]
    onehot = onehot[0] if eb == 1 else jnp.concatenate(onehot, axis=0)
    xg = _mm(onehot, h_ref[...]).astype(BF16)
    for e in range(eb):
        o_ref[e, 0] = xg[e * cap:(e + 1) * cap]


def _gather(slot, h2, n_sets, n):
    ne, t = slot.shape
    d = h2.shape[1]
    cap = EC_CAPACITY_FACTOR * n // N_EXPERTS
    eb = min(ne, max(1, GATHER_ROWS // cap))
    td = min(d, 512)
    slot3 = slot.reshape(ne, 1, t)
    return pl.pallas_call(
        _gather_kernel,
        out_shape=jax.ShapeDtypeStruct((ne, n_sets, cap, d), BF16),
        grid=(n_sets, d // td, ne // eb),
        in_specs=[pl.BlockSpec((eb, 1, n), lambda b, j, e: (e, 0, b)),
                  pl.BlockSpec((n, td), lambda b, j, e: (b, j))],
        out_specs=pl.BlockSpec((eb, 1, cap, td), lambda b, j, e: (e, b, 0, j)),
        compiler_params=_cp(3),
        name="moe_gather",
    )(slot3, h2)


def _ffn_kernel(x_ref, g_ref, wg_ref, wu_ref, wd_ref, y_ref, acc_ref):
    f = pl.program_id(2)

    @pl.when(f == 0)
    def _():
        acc_ref[...] = jnp.zeros_like(acc_ref)

    x = x_ref[0]
    a = _mm(x, wg_ref[0].astype(BF16))
    u = _mm(x, wu_ref[0].astype(BF16))
    hmid = (a * jax.nn.sigmoid(a) * u).astype(BF16)
    acc_ref[...] += _mm(hmid, wd_ref[0].astype(BF16))

    @pl.when(f == pl.num_programs(2) - 1)
    def _():
        y_ref[0] = (acc_ref[...] * g_ref[0]).astype(BF16)


FFN_ROWS = 1024
FFN_COLS = 512


def _expert_ffn(xg, gslot, w_gate, w_up, w_down):
    ne, m, d = xg.shape
    ff = w_gate.shape[-1]
    tf = min(ff, FFN_COLS)
    tm = min(m, FFN_ROWS)
    return pl.pallas_call(
        _ffn_kernel,
        out_shape=jax.ShapeDtypeStruct((ne, m, d), BF16),
        grid=(ne, m // tm, ff // tf),
        in_specs=[pl.BlockSpec((1, tm, d), lambda e, i, f: (e, i, 0)),
                  pl.BlockSpec((1, tm, 1), lambda e, i, f: (e, i, 0)),
                  pl.BlockSpec((1, d, tf), lambda e, i, f: (e, 0, f)),
                  pl.BlockSpec((1, d, tf), lambda e, i, f: (e, 0, f)),
                  pl.BlockSpec((1, tf, d), lambda e, i, f: (e, f, 0))],
        out_specs=pl.BlockSpec((1, tm, d), lambda e, i, f: (e, i, 0)),
        scratch_shapes=[pltpu.VMEM((tm, d), F32)],
        compiler_params=_cp(3),
        name="moe_ffn",
    )(xg, gslot, w_gate, w_up, w_down)


def _scatter_kernel(x_ref, slott_ref, y_ref, gate2_ref, o_ref):
    td = x_ref.shape[-1]
    ne, _, cap, _ = y_ref.shape
    tn = x_ref.shape[0]
    lane = lax.broadcasted_iota(jnp.int32, (tn, cap), 1)
    st = slott_ref[...]
    moe = jnp.zeros((tn, td), F32)
    for e in range(ne):
        onehot = jnp.where(st[:, e:e + 1] == lane, 1.0, 0.0).astype(BF16)
        moe = moe + _mm(onehot, y_ref[e, 0])
    o_ref[...] = x_ref[...] + gate2_ref[0] * moe


def _scatter_residual(x1, slot_t, y4, mod3, mod_base, n_sets, n, latent):
    t, d = x1.shape
    ne, _, cap, _ = y4.shape
    tn = min(n, 256)
    td = min(d, 512)
    npt = n // tn
    g2_blk = 5 * (d // td)
    if latent:
        mod_idx = lambda b, j, i: (mod_base + b, 0, g2_blk + j)
    else:
        mod_idx = lambda b, j, i: (mod_base, 0, g2_blk + j)
    return pl.pallas_call(
        _scatter_kernel,
        out_shape=jax.ShapeDtypeStruct((t, d), F32),
        grid=(n_sets, d // td, npt),
        in_specs=[pl.BlockSpec((tn, td), lambda b, j, i: (b * npt + i, j)),
                  pl.BlockSpec((tn, ne), lambda b, j, i: (b * npt + i, 0)),
                  pl.BlockSpec((ne, 1, cap, td), lambda b, j, i: (0, b, 0, j)),
                  pl.BlockSpec((1, 1, td), mod_idx)],
        out_specs=pl.BlockSpec((tn, td), lambda b, j, i: (b * npt + i, j)),
        compiler_params=_cp(3),
        name="moe_scatter",
    )(x1, slot_t, y4, mod3)


def _mixer_ctx(x2d, batch, seq, mod3, lp):
    hy, qna, qg, nak, nav, gk, gv = _in_projection(
        x2d, mod3, 0, batch, seq, lp["norm1_g"], lp["w_in_bf"], lp["head_gains"], latent=False)
    y_hy = _hyena(hy.reshape(batch, seq, HY_IN), lp)
    ona, og = _ctx_attention(qna, qg, nak, nav, gk, gv)
    x1, h2, aff = _out_projection(x2d, y_hy, ona, og, mod3, 0, seq, lp["norm2_g"], lp["w_out_bf"],
                                  lp["rw_hi"], lp["rw_lo"], latent=False)
    return x1, h2, aff, (nak, nav, gk, gv)


def _mixer_lat(x2d, batch, seq, mod3, lp, caches, layer):
    hy, qna, kna, vna, qg, kg, vg = _in_projection(
        x2d, mod3, 1, batch, seq, lp["norm1_g"], lp["w_in_bf"], lp["head_gains"], latent=True)
    y_hy = _hyena(hy.reshape(batch, seq, HY_IN), lp)
    na_kc, na_vc, g_kc, g_vc = caches
    ona = _na_latent(qna, kna, vna, na_kc, na_vc, layer, lp["na_rpb"], batch, seq)
    og = _gqa_latent(qg, kg, vg, g_kc, g_vc, layer, batch, seq)
    x1, h2, aff = _out_projection(x2d, y_hy, ona, og, mod3, 1, seq, lp["norm2_g"], lp["w_out_bf"],
                                  lp["rw_hi"], lp["rw_lo"], latent=True)
    return x1, h2, aff


def _moe(parts, mod3, lp):
    routed = []
    for x1, h2, aff, n_sets, n, mod_base, latent in parts:
        slot, slot_t, gslot = _route(aff, n_sets, n)
        xg = _gather(slot, h2, n_sets, n)
        routed.append((slot_t, gslot, xg))
    ne = N_EXPERTS
    d = parts[0][0].shape[1]
    xg_all = jnp.concatenate([xg.reshape(ne, -1, d) for _, _, xg in routed], axis=1)
    gs_all = jnp.concatenate([jnp.swapaxes(g, 0, 1).reshape(ne, -1, 1) for _, g, _ in routed], axis=1)
    y_all = _expert_ffn(xg_all, gs_all, lp["exp_w_gate"], lp["exp_w_up"], lp["exp_w_down"])
    outs, off = [], 0
    for (x1, h2, aff, n_sets, n, mod_base, latent), (slot_t, gslot, xg) in zip(parts, routed):
        m = xg.shape[1] * xg.shape[2]
        y4 = y_all[:, off:off + m].reshape(xg.shape)
        off += m
        outs.append(_scatter_residual(x1, slot_t, y4, mod3, mod_base, n_sets, n, latent))
    return outs


def kernel(x_prompt, x_sample, cache_na_k, cache_na_v, cache_gqa_k, cache_gqa_v, c, c_ctx, ada_w, ada_b, norm1_g, norm2_g, w_in, w_out, hy_short_w, hy_short_b, hy_f_w1, hy_f_b1, hy_f_w2, hy_f_b2, hy_f_w3, hy_f_freq, hy_bias, na_q_g, na_k_g, na_rpb, gqa_q_g, gqa_k_g, router_w, exp_w_gate, exp_w_up, exp_w_down):
    batch, seq, d = x_prompt.shape
    dbatch, dseq, _ = x_sample.shape
    depth = ada_w.shape[0]
    rows = 8 * ((1 + dbatch + 7) // 8)
    cvec = jnp.zeros((rows, d), F32).at[0].set(c_ctx).at[1:1 + dbatch].set(c)
    mod_all = _modulation(cvec, ada_w, ada_b)
    yp = x_prompt.reshape(batch * seq, d)
    ys = x_sample.reshape(dbatch * dseq, d)
    w_in_bf = w_in.astype(BF16)
    w_out_bf = w_out.astype(BF16)
    rw_t = jnp.swapaxes(router_w, 1, 2)
    rw_hi = rw_t.astype(BF16)
    rw_lo = (rw_t - rw_hi.astype(F32)).astype(BF16)
    new_kv = [[], [], [], []]
    for l in range(depth):
        lp = {
            "norm1_g": norm1_g[l], "norm2_g": norm2_g[l], "w_in_bf": w_in_bf[l], "w_out_bf": w_out_bf[l],
            "hy_short_w": hy_short_w[l], "hy_short_b": hy_short_b[l],
            "hy_f_w1": hy_f_w1[l], "hy_f_b1": hy_f_b1[l], "hy_f_w2": hy_f_w2[l], "hy_f_b2": hy_f_b2[l],
            "hy_f_w3": hy_f_w3[l], "hy_f_freq": hy_f_freq[l], "hy_bias": hy_bias[l],
            "head_gains": jnp.stack([na_q_g[l], na_k_g[l], gqa_q_g[l], gqa_k_g[l]]),
            "na_rpb": na_rpb[l], "rw_hi": rw_hi[l], "rw_lo": rw_lo[l],
            "exp_w_gate": exp_w_gate[l], "exp_w_up": exp_w_up[l], "exp_w_down": exp_w_down[l],
        }
        mod3 = mod_all[l].reshape(rows, 1, 6 * d)
        xp1, hp2, affp, kv = _mixer_ctx(yp, batch, seq, mod3, lp)
        for dst, src in zip(new_kv, kv):
            dst.append(src)
        xs1, hs2, affs = _mixer_lat(ys, dbatch, dseq, mod3, lp,
                                    (cache_na_k, cache_na_v, cache_gqa_k, cache_gqa_v), l)
        yp, ys = _moe([(xp1, hp2, affp, batch, seq, 0, False), (xs1, hs2, affs, dbatch, dseq, 1, True)],
                      mod3, lp)
    outs = [jnp.stack(v, axis=1) for v in new_kv]
    return (yp.reshape(batch, seq, d), ys.reshape(dbatch, dseq, d), outs[0], outs[1], outs[2], outs[3])
```

```python
import functools
import math

import numpy as np
import jax
import jax.numpy as jnp
from jax import lax
from jax.experimental import pallas as pl
from jax.experimental.pallas import tpu as pltpu

F32 = jnp.float32
BF16 = jnp.bfloat16

HEAD_DIM = 128
GRID_W = 64
HY_WIDTH = 512
HY_ORDER = 2
HY_IN = (HY_ORDER + 1) * HY_WIDTH
HY_POS_EMB = 33
HY_DECAY_TARGET = 1e-2
HY_FAST_PCT = 0.3
HY_SLOW_PCT = 1.5
NA_HEADS = 6
NA_WIN_H = 8
NA_WIN_W = 16
GQA_Q_HEADS = 6
GQA_KV_HEADS = 2
GQA_GROUP = GQA_Q_HEADS // GQA_KV_HEADS
ROPE_THETA = 10000.0
N_EXPERTS = 16
EC_CAPACITY_FACTOR = 2
NORM_EPS = 1e-6
MASK_VALUE = -1e30
NA_W = NA_HEADS * HEAD_DIM
GQ_W = GQA_Q_HEADS * HEAD_DIM
GKV_W = GQA_KV_HEADS * HEAD_DIM

LANES = 128
VMEM_LIMIT_BYTES = 56 * 1024 * 1024
LOG2_E = math.log2(math.e)
DFT_INNER = 128
DIRECT_DFT_MAX_LEN = 512


def _cp(n_axes, vmem=VMEM_LIMIT_BYTES):
    return pltpu.CompilerParams(dimension_semantics=("arbitrary",) * n_axes, vmem_limit_bytes=vmem)


def _nt(a, b):
    return lax.dot_general(a, b, (((1,), (1,)), ((), ())), preferred_element_type=F32)


def _mm(a, b):
    return jnp.dot(a, b, preferred_element_type=F32)


def _mm_hi(a, b):
    return jnp.dot(a, b, preferred_element_type=F32, precision=lax.Precision.HIGHEST)


def _mod_kernel(c_ref, w_ref, b_ref, o_ref):
    c = c_ref[...]
    s = c * jax.nn.sigmoid(c)
    o_ref[0] = _mm(s.astype(BF16), w_ref[0].astype(BF16)) + b_ref[0]


def _modulation(cvec, ada_w, ada_b):
    depth, d, n = ada_w.shape
    rows = cvec.shape[0]
    tn = min(n, 512)
    return pl.pallas_call(
        _mod_kernel,
        out_shape=jax.ShapeDtypeStruct((depth, rows, n), F32),
        grid=(depth, n // tn),
        in_specs=[
            pl.BlockSpec((rows, d), lambda l, j: (0, 0)),
            pl.BlockSpec((1, d, tn), lambda l, j: (l, 0, j)),
            pl.BlockSpec((1, 1, tn), lambda l, j: (l, 0, j)),
        ],
        out_specs=pl.BlockSpec((1, rows, tn), lambda l, j: (l, 0, j)),
        compiler_params=_cp(2),
        name="modulation",
    )(cvec, ada_w, ada_b.reshape(depth, 1, n))


def _head_norm(t, g):
    return t * lax.rsqrt(jnp.mean(t * t, axis=-1, keepdims=True) + NORM_EPS) * g


def _rope(t, cos, sin):
    swapped = jnp.where((lax.broadcasted_iota(jnp.int32, t.shape, 1) % 64) < 32,
                        pltpu.roll(t, HEAD_DIM - 32, 1), pltpu.roll(t, 32, 1))
    return t * cos + swapped * sin


def _inproj_kernel(latent, x_ref, mod_ref, g1_ref, w_ref, hg_ref, *rest):
    if latent:
        cos_ref, sin_ref, hy_ref, qna_ref, kna_ref, vna_ref, qg_ref, kg_ref, vg_ref = rest
    else:
        hy_ref, qna_ref, qg_ref, nak_ref, nav_ref, gk_ref, gv_ref = rest
    d = x_ref.shape[-1]
    x = x_ref[...]
    mod = mod_ref[0]
    shift, scale = mod[:, 0:d], mod[:, d:2 * d]
    ms = jnp.mean(x * x, axis=-1, keepdims=True)
    h = (x * lax.rsqrt(ms + NORM_EPS) * g1_ref[...]) * (1.0 + scale) + shift
    hb = h.astype(BF16)

    def proj(c0, n):
        return _mm(hb, w_ref[:, c0:c0 + n])

    hy_ref[...] = proj(0, HY_IN)
    q_scale = HEAD_DIM ** -0.5 * LOG2_E
    g_naq, g_nak = hg_ref[0:1, :], hg_ref[1:2, :]
    g_gq, g_gk = hg_ref[2:3, :], hg_ref[3:4, :]
    c = HY_IN
    z_naq = proj(c, NA_W)
    z_nak = proj(c + NA_W, NA_W)
    z_nav = proj(c + 2 * NA_W, NA_W)
    c += 3 * NA_W
    z_gq = proj(c, GQ_W)
    z_gk = proj(c + GQ_W, GKV_W)
    z_gv = proj(c + GQ_W + GKV_W, GKV_W)
    if latent:
        cos, sin = cos_ref[...], sin_ref[...]
    seq = None if latent else nak_ref.shape[2]
    nb = None if latent else nak_ref.shape[0]
    for hd in range(NA_HEADS):
        sl = slice(hd * HEAD_DIM, (hd + 1) * HEAD_DIM)
        qna_ref[:, sl] = (_head_norm(z_naq[:, sl], g_naq) * q_scale).astype(BF16)
        k = _head_norm(z_nak[:, sl], g_nak)
        v = z_nav[:, sl]
        if latent:
            kna_ref[:, sl] = k.astype(BF16)
            vna_ref[:, sl] = v.astype(BF16)
        else:
            for b in range(nb):
                nak_ref[b, hd] = k[b * seq:(b + 1) * seq]
                nav_ref[b, hd] = v[b * seq:(b + 1) * seq]
    for hd in range(GQA_Q_HEADS):
        sl = slice(hd * HEAD_DIM, (hd + 1) * HEAD_DIM)
        q = _head_norm(z_gq[:, sl], g_gq)
        if latent:
            q = _rope(q, cos, sin)
        qg_ref[:, sl] = (q * q_scale).astype(BF16)
    for hd in range(GQA_KV_HEADS):
        sl = slice(hd * HEAD_DIM, (hd + 1) * HEAD_DIM)
        k = _head_norm(z_gk[:, sl], g_gk)
        v = z_gv[:, sl]
        if latent:
            kg_ref[:, sl] = _rope(k, cos, sin).astype(BF16)
            vg_ref[:, sl] = v.astype(BF16)
        else:
            for b in range(nb):
                gk_ref[b, hd] = k[b * seq:(b + 1) * seq]
                gv_ref[b, hd] = v[b * seq:(b + 1) * seq]


def _rope_tables(n):
    pos = np.arange(n)
    row = (pos // GRID_W).astype(np.float64)
    col = (pos % GRID_W).astype(np.float64)
    quarter = HEAD_DIM // 4
    inv = ROPE_THETA ** (-np.arange(quarter, dtype=np.float64) / quarter)
    ang_r = row[:, None] * inv[None, :]
    ang_c = col[:, None] * inv[None, :]
    cos = np.concatenate([np.cos(ang_r), np.cos(ang_r), np.cos(ang_c), np.cos(ang_c)], axis=-1)
    sin = np.concatenate([-np.sin(ang_r), np.sin(ang_r), -np.sin(ang_c), np.sin(ang_c)], axis=-1)
    return jnp.asarray(cos, F32), jnp.asarray(sin, F32)


def _in_projection(x2d, mod3, mod_base, batch, seq, g1, w_in_bf, head_gains, latent):
    t, d = x2d.shape
    tm = min(512, seq) if latent else min(512, t)
    if not latent:
        tm = max(tm, seq)
    nt_ = t // tm
    tiles_per_batch = seq // tm if latent else None
    wide = w_in_bf.shape[1]
    if latent:
        mod_idx = lambda i: (mod_base + i // tiles_per_batch, 0, 0)
    else:
        mod_idx = lambda i: (mod_base, 0, 0)
    in_specs = [
        pl.BlockSpec((tm, d), lambda i: (i, 0)),
        pl.BlockSpec((1, 1, mod3.shape[-1]), mod_idx),
        pl.BlockSpec((1, d), lambda i: (0, 0)),
        pl.BlockSpec((d, wide), lambda i: (0, 0), pipeline_mode=pl.Buffered(1)),
        pl.BlockSpec((4, HEAD_DIM), lambda i: (0, 0)),
    ]
    args = [x2d, mod3, g1.reshape(1, d), w_in_bf, head_gains]
    row_spec = lambda w: pl.BlockSpec((tm, w), lambda i: (i, 0))
    if latent:
        cos, sin = _rope_tables(seq)
        in_specs += [pl.BlockSpec((tm, HEAD_DIM), lambda i: (i % tiles_per_batch, 0))] * 2
        args += [cos, sin]
        out_shape = [
            jax.ShapeDtypeStruct((t, HY_IN), F32),
            jax.ShapeDtypeStruct((t, NA_W), BF16), jax.ShapeDtypeStruct((t, NA_W), BF16),
            jax.ShapeDtypeStruct((t, NA_W), BF16), jax.ShapeDtypeStruct((t, GQ_W), BF16),
            jax.ShapeDtypeStruct((t, GKV_W), BF16), jax.ShapeDtypeStruct((t, GKV_W), BF16),
        ]
        out_specs = [row_spec(HY_IN), row_spec(NA_W), row_spec(NA_W), row_spec(NA_W),
                     row_spec(GQ_W), row_spec(GKV_W), row_spec(GKV_W)]
    else:
        nb = tm // seq
        kv_spec = lambda hh: pl.BlockSpec((nb, hh, seq, HEAD_DIM), lambda i: (i, 0, 0, 0))
        out_shape = [
            jax.ShapeDtypeStruct((t, HY_IN), F32),
            jax.ShapeDtypeStruct((t, NA_W), BF16), jax.ShapeDtypeStruct((t, GQ_W), BF16),
            jax.ShapeDtypeStruct((batch, NA_HEADS, seq, HEAD_DIM), F32),
            jax.ShapeDtypeStruct((batch, NA_HEADS, seq, HEAD_DIM), F32),
            jax.ShapeDtypeStruct((batch, GQA_KV_HEADS, seq, HEAD_DIM), F32),
            jax.ShapeDtypeStruct((batch, GQA_KV_HEADS, seq, HEAD_DIM), F32),
        ]
        out_specs = [row_spec(HY_IN), row_spec(NA_W), row_spec(GQ_W),
                     kv_spec(NA_HEADS), kv_spec(NA_HEADS), kv_spec(GQA_KV_HEADS), kv_spec(GQA_KV_HEADS)]
    return pl.pallas_call(
        functools.partial(_inproj_kernel, latent),
        out_shape=out_shape,
        grid=(nt_,),
        in_specs=in_specs,
        out_specs=out_specs,
        compiler_params=_cp(1),
        name="in_proj_lat" if latent else "in_proj_ctx",
    )(*args)


def _shortconv_kernel(x_ref, w_ref, b_ref, o_ref):
    x = x_ref[0]
    n = x.shape[0]
    rows = lax.broadcasted_iota(jnp.int32, x.shape, 0)
    prev = jnp.where(rows == 0, 0.0, pltpu.roll(x, 1, 0))
    nxt = jnp.where(rows == n - 1, 0.0, pltpu.roll(x, n - 1, 0))
    o_ref[0, 0] = prev * w_ref[0:1, :] + x * w_ref[1:2, :] + nxt * w_ref[2:3, :] + b_ref[...]


def _short_conv(hy3, sw, sb):
    b, l, _ = hy3.shape
    tc = 256
    per = HY_WIDTH // tc
    return pl.pallas_call(
        _shortconv_kernel,
        out_shape=jax.ShapeDtypeStruct((HY_ORDER + 1, b, l, HY_WIDTH), F32),
        grid=(b, HY_IN // tc),
        in_specs=[
            pl.BlockSpec((1, l, tc), lambda i, q: (i, 0, q)),
            pl.BlockSpec((3, tc), lambda i, q: (0, q)),
            pl.BlockSpec((1, tc), lambda i, q: (0, q)),
        ],
        out_specs=pl.BlockSpec((1, 1, l, tc), lambda i, q: (q // per, i, 0, q % per)),
        compiler_params=_cp(2),
        name="hy_short_conv",
    )(hy3, sw, sb.reshape(1, HY_IN))


def _filter_kernel(z_ref, w1_ref, b1_ref, w2_ref, b2_ref, w3_ref, fr_ref, dl_ref, f_ref, n_ref):
    i = pl.program_id(0)
    z = z_ref[...]
    sf = fr_ref[...]
    hid = jnp.sin(sf * (_mm_hi(z, w1_ref[...]) + b1_ref[...]))
    hid = jnp.sin(sf * (_mm_hi(hid, w2_ref[...]) + b2_ref[...]))
    filt = _mm_hi(hid, w3_ref[...])
    t = z[:, 0:1]
    decay = jnp.exp(-t * dl_ref[...])
    decay = jnp.concatenate([decay] * (2 * HY_ORDER), axis=-1)
    filt = filt * decay
    half = HY_ORDER * HY_WIDTH
    rows = lax.broadcasted_iota(jnp.int32, filt.shape, 0) + i * filt.shape[0]
    cols = lax.broadcasted_iota(jnp.int32, filt.shape, 1)
    filt = jnp.where((rows == 0) & (cols >= half), 0.0, filt)
    f_ref[...] = filt
    part = jnp.sum(jnp.abs(filt), axis=0, keepdims=True)

    @pl.when(i == 0)
    def _():
        n_ref[...] = jnp.zeros_like(n_ref)

    n_ref[...] += part[:, :half] + part[:, half:]


def _filter_embedding(l):
    t = np.linspace(0.0, 1.0, l, dtype=np.float64)[:, None]
    bands = (HY_POS_EMB - 1) // 2
    fr = np.linspace(1e-4, bands - 1, bands, dtype=np.float64)[None, :]
    w = 2.0 * math.pi * np.arange(l, dtype=np.float64)[:, None] / l
    z = np.concatenate([t, np.cos(fr * w), -np.sin(fr * w)], axis=-1)
    zp = np.zeros((l, LANES), np.float32)
    zp[:, :HY_POS_EMB] = z
    min_d = abs(math.log(HY_DECAY_TARGET) / HY_SLOW_PCT)
    max_d = abs(math.log(HY_DECAY_TARGET) / HY_FAST_PCT)
    deltas = np.linspace(min_d, max_d, HY_WIDTH, dtype=np.float64)[None, :]
    return jnp.asarray(zp, F32), jnp.asarray(deltas, F32)


def _hyena_filters(l, w1, b1, w2, b2, w3, freq):
    zp, deltas = _filter_embedding(l)
    hid = w1.shape[1]
    w1p = jnp.zeros((LANES, hid), F32).at[:HY_POS_EMB].set(w1)
    tl = min(l, 512)
    wide = w3.shape[1]
    full = lambda a: pl.BlockSpec(a.shape, lambda i: (0,) * a.ndim)
    b1r, b2r, frr = b1.reshape(1, hid), b2.reshape(1, hid), freq.reshape(1, hid)
    return pl.pallas_call(
        _filter_kernel,
        out_shape=[jax.ShapeDtypeStruct((l, wide), F32),
                   jax.ShapeDtypeStruct((1, HY_ORDER * HY_WIDTH), F32)],
        grid=(l // tl,),
        in_specs=[pl.BlockSpec((tl, LANES), lambda i: (i, 0)), full(w1p), full(b1r), full(w2), full(b2r),
                  full(w3), full(frr), full(deltas)],
        out_specs=[pl.BlockSpec((tl, wide), lambda i: (i, 0)),
                   pl.BlockSpec((1, HY_ORDER * HY_WIDTH), lambda i: (0, 0))],
        compiler_params=_cp(1),
        name="hy_filters",
    )(zp, w1p, b1r, w2, b2r, w3, frr, deltas)


def _bf16_const(a):
    return jnp.asarray(a, F32).astype(BF16)


def _dft_plan(l):
    n = 2 * l
    n2_len = 1 if l <= DIRECT_DFT_MAX_LEN else DFT_INNER
    n1_len = n // n2_len
    k1 = np.arange(n1_len, dtype=np.float64)[:, None]
    n1 = np.arange(n1_len // 2, dtype=np.float64)[None, :]
    ang = 2.0 * np.pi * k1 * n1 / n1_len
    plan = {
        "n1": n1_len, "n2": n2_len,
        "fs": _bf16_const(np.concatenate([np.cos(ang), -np.sin(ang)], axis=0)),
        "gr": _bf16_const(np.cos(ang).T / n),
        "gi": _bf16_const(-np.sin(ang).T / n),
    }
    if n2_len > 1:
        k2 = np.arange(n2_len, dtype=np.float64)
        a2 = 2.0 * np.pi * np.outer(k2, k2) / n2_len
        fr, fi = np.cos(a2), -np.sin(a2)
        plan["d2"] = _bf16_const(np.block([[fr, -fi], [fi, fr]]))
        plan["d2c"] = _bf16_const(np.block([[fr, fi], [-fi, fr]]))
        at = 2.0 * np.pi * np.arange(n1_len, dtype=np.float64)[:, None] * k2[None, :] / n
        plan["twr"] = jnp.asarray(np.repeat(np.cos(at)[:, :, None], LANES, axis=2), F32)
        plan["twi"] = jnp.asarray(np.repeat(-np.sin(at)[:, :, None], LANES, axis=2), F32)
    return plan


def _dft1_kernel(z_ref, fs_ref, ar_ref, ai_ref):
    a = _mm(fs_ref[...], z_ref[0, 0].astype(BF16))
    n1 = ar_ref.shape[1]
    ar_ref[0] = a[:n1].astype(BF16)
    ai_ref[0] = a[n1:].astype(BF16)


def _dft_stage1(z4, part, plan):
    _, b, kk, nc = z4.shape
    n1 = plan["n1"]
    tn = min(nc, 4096)
    return pl.pallas_call(
        _dft1_kernel,
        out_shape=[jax.ShapeDtypeStruct((b, n1, nc), BF16)] * 2,
        grid=(b, nc // tn),
        in_specs=[pl.BlockSpec((1, 1, kk, tn), lambda i, j: (part, i, 0, j)),
                  pl.BlockSpec((2 * n1, kk), lambda i, j: (0, 0))],
        out_specs=[pl.BlockSpec((1, n1, tn), lambda i, j: (i, 0, j))] * 2,
        compiler_params=_cp(2),
        name="hy_dft_outer",
    )(z4, plan["fs"])


def _twiddle_inner_dft(ar, ai, twr, twi, d2):
    c = ar.shape[-1]
    twr = jnp.concatenate([twr] * (c // LANES), axis=-1)
    twi = jnp.concatenate([twi] * (c // LANES), axis=-1)
    zr = ar * twr - ai * twi
    zi = ar * twi + ai * twr
    x = _mm(d2, jnp.concatenate([zr, zi], axis=0).astype(BF16))
    n2 = ar.shape[0]
    return x[:n2], x[n2:], twr, twi


def _spec_filter_kernel(two_stage, kb, ar_ref, ai_ref, nrm_ref, *rest):
    if two_stage:
        twr_ref, twi_ref, d2_ref, kr_ref, ki_ref = rest
    else:
        kr_ref, ki_ref = rest
    half = HY_ORDER * HY_WIDTH
    inv = 1.0 / nrm_ref[...]

    def combine(xr, xi):
        return (xr[:, :half] + xr[:, half:]) * inv, (xi[:, :half] - xi[:, half:]) * inv

    if two_stage:
        for j in range(kb):
            xr, xi, _, _ = _twiddle_inner_dft(ar_ref[0, j].astype(F32), ai_ref[0, j].astype(F32),
                                              twr_ref[j], twi_ref[j], d2_ref[...])
            kr, ki = combine(xr, xi)
            for o in range(HY_ORDER):
                kr_ref[o, j] = kr[:, o * HY_WIDTH:(o + 1) * HY_WIDTH]
                ki_ref[o, j] = ki[:, o * HY_WIDTH:(o + 1) * HY_WIDTH]
    else:
        kr, ki = combine(ar_ref[0].astype(F32), ai_ref[0].astype(F32))
        for o in range(HY_ORDER):
            kr_ref[o] = kr[:, o * HY_WIDTH:(o + 1) * HY_WIDTH]
            ki_ref[o] = ki[:, o * HY_WIDTH:(o + 1) * HY_WIDTH]


def _filter_spectrum(filt, norm, plan):
    l, wide = filt.shape
    n1, n2 = plan["n1"], plan["n2"]
    ar, ai = _dft_stage1(filt.reshape(1, 1, n1 // 2, n2 * wide), 0, plan)
    two_stage = n2 > 1
    if two_stage:
        kb = 8
        ar = ar.reshape(1, n1, n2, wide)
        ai = ai.reshape(1, n1, n2, wide)
        a_spec = pl.BlockSpec((1, kb, n2, wide), lambda i: (0, i, 0, 0))
        in_specs = [a_spec, a_spec, pl.BlockSpec(norm.shape, lambda i: (0, 0)),
                    pl.BlockSpec((kb, n2, LANES), lambda i: (i, 0, 0)),
                    pl.BlockSpec((kb, n2, LANES), lambda i: (i, 0, 0)),
                    pl.BlockSpec((2 * n2, 2 * n2), lambda i: (0, 0))]
        args = [ar, ai, norm, plan["twr"], plan["twi"], plan["d2"]]
        out_shape = [jax.ShapeDtypeStruct((HY_ORDER, n1, n2, HY_WIDTH), F32)] * 2
        out_specs = [pl.BlockSpec((HY_ORDER, kb, n2, HY_WIDTH), lambda i: (0, i, 0, 0))] * 2
    else:
        kb = min(n1, 256)
        a_spec = pl.BlockSpec((1, kb, wide), lambda i: (0, i, 0))
        in_specs = [a_spec, a_spec, pl.BlockSpec(norm.shape, lambda i: (0, 0))]
        args = [ar, ai, norm]
        out_shape = [jax.ShapeDtypeStruct((HY_ORDER, n1, HY_WIDTH), F32)] * 2
        out_specs = [pl.BlockSpec((HY_ORDER, kb, HY_WIDTH), lambda i: (0, i, 0))] * 2
    return pl.pallas_call(
        functools.partial(_spec_filter_kernel, two_stage, kb),
        out_shape=out_shape,
        grid=(n1 // kb,),
        in_specs=in_specs,
        out_specs=out_specs,
        compiler_params=_cp(1),
        name="hy_filter_spectrum",
    )(*args)


def _spec_mul_kernel(two_stage, kb, ar_ref, ai_ref, kr_ref, ki_ref, *rest):
    if two_stage:
        twr_ref, twi_ref, d2_ref, d2c_ref, br_ref, bi_ref = rest
        for j in range(kb):
            xr, xi, twr, twi = _twiddle_inner_dft(ar_ref[0, j].astype(F32), ai_ref[0, j].astype(F32),
                                                  twr_ref[j], twi_ref[j], d2_ref[...])
            kr, ki = kr_ref[0, j], ki_ref[0, j]
            yr = xr * kr - xi * ki
            yi = xr * ki + xi * kr
            bm = _mm(d2c_ref[...], jnp.concatenate([yr, yi], axis=0).astype(BF16))
            n2 = xr.shape[0]
            br, bi = bm[:n2], bm[n2:]
            br_ref[0, j] = (br * twr + bi * twi).astype(BF16)
            bi_ref[0, j] = (bi * twr - br * twi).astype(BF16)
    else:
        br_ref, bi_ref = rest
        xr, xi = ar_ref[0].astype(F32), ai_ref[0].astype(F32)
        kr, ki = kr_ref[0], ki_ref[0]
        br_ref[0] = (xr * kr - xi * ki).astype(BF16)
        bi_ref[0] = (xr * ki + xi * kr).astype(BF16)


def _spectrum_multiply(ar, ai, kr, ki, order, plan):
    b, n1, nc = ar.shape
    n2 = plan["n2"]
    c = nc // n2
    two_stage = n2 > 1
    if two_stage:
        kb = 8
        ar = ar.reshape(b, n1, n2, c)
        ai = ai.reshape(b, n1, n2, c)
        a_spec = pl.BlockSpec((1, kb, n2, c), lambda i, j: (i, j, 0, 0))
        k_spec = pl.BlockSpec((1, kb, n2, c), lambda i, j: (order, j, 0, 0))
        tw_spec = pl.BlockSpec((kb, n2, LANES), lambda i, j: (j, 0, 0))
        d_spec = pl.BlockSpec((2 * n2, 2 * n2), lambda i, j: (0, 0))
        in_specs = [a_spec, a_spec, k_spec, k_spec, tw_spec, tw_spec, d_spec, d_spec]
        args = [ar, ai, kr, ki, plan["twr"], plan["twi"], plan["d2"], plan["d2c"]]
        out_shape = [jax.ShapeDtypeStruct((b, n1, n2, c), BF16)] * 2
        out_specs = [a_spec, a_spec]
    else:
        kb = min(n1, 256)
        a_spec = pl.BlockSpec((1, kb, c), lambda i, j: (i, j, 0))
        k_spec = pl.BlockSpec((1, kb, c), lambda i, j: (order, j, 0))
        in_specs = [a_spec, a_spec, k_spec, k_spec]
        args = [ar, ai, kr, ki]
        out_shape = [jax.ShapeDtypeStruct((b, n1, c), BF16)] * 2
        out_specs = [a_spec, a_spec]
    br, bi = pl.pallas_call(
        functools.partial(_spec_mul_kernel, two_stage, kb),
        out_shape=out_shape,
        grid=(b, n1 // kb),
        in_specs=in_specs,
        out_specs=out_specs,
        compiler_params=_cp(2),
        name="hy_spectrum_mul",
    )(*args)
    return br.reshape(b, n1, nc), bi.reshape(b, n1, nc)


def _idft1_kernel(br_ref, bi_ref, gr_ref, gi_ref, z_ref, x_ref, bias_ref, o_ref):
    conv = _mm(gr_ref[...], br_ref[0]) + _mm(gi_ref[...], bi_ref[0])
    z = z_ref[0, 0]
    o_ref[0, 0] = (x_ref[0, 0] * (conv + z * bias_ref[...])).astype(o_ref.dtype)


def _idft_gate(br, bi, plan, z4, z_part, u4, x_part, bias_row, out_dtype):
    b, n1, nc = br.shape
    kk = n1 // 2
    tn = min(nc, 4096)
    return pl.pallas_call(
        _idft1_kernel,
        out_shape=jax.ShapeDtypeStruct((1, b, kk, nc), out_dtype),
        grid=(b, nc // tn),
        in_specs=[pl.BlockSpec((1, n1, tn), lambda i, j: (i, 0, j)),
                  pl.BlockSpec((1, n1, tn), lambda i, j: (i, 0, j)),
                  pl.BlockSpec((kk, n1), lambda i, j: (0, 0)),
                  pl.BlockSpec((kk, n1), lambda i, j: (0, 0)),
                  pl.BlockSpec((1, 1, kk, tn), lambda i, j: (z_part, i, 0, j)),
                  pl.BlockSpec((1, 1, kk, tn), lambda i, j: (x_part, i, 0, j)),
                  pl.BlockSpec((1, tn), lambda i, j: (0, j))],
        out_specs=pl.BlockSpec((1, 1, kk, tn), lambda i, j: (0, i, 0, j)),
        compiler_params=_cp(2),
        name="hy_idft_gate",
    )(br, bi, plan["gr"], plan["gi"], z4, u4, bias_row)


def _hyena(hy3, lp):
    b, l, _ = hy3.shape
    plan = _dft_plan(l)
    n1, n2 = plan["n1"], plan["n2"]
    filt, norm = _hyena_filters(l, lp["hy_f_w1"], lp["hy_f_b1"], lp["hy_f_w2"], lp["hy_f_b2"],
                                lp["hy_f_w3"], lp["hy_f_freq"])
    kr, ki = _filter_spectrum(filt, norm, plan)
    u = _short_conv(hy3, lp["hy_short_w"], lp["hy_short_b"])
    u4 = u.reshape(HY_ORDER + 1, b, n1 // 2, n2 * HY_WIDTH)
    z4, z_part = u4, 0
    for o in range(HY_ORDER):
        ar, ai = _dft_stage1(z4, z_part, plan)
        br, bi = _spectrum_multiply(ar, ai, kr, ki, o, plan)
        bias_row = jnp.tile(lp["hy_bias"][o].reshape(1, HY_WIDTH), (1, n2))
        last = o == HY_ORDER - 1
        z4 = _idft_gate(br, bi, plan, z4, z_part, u4, o + 1, bias_row, BF16 if last else F32)
        z_part = 0
    return z4.reshape(b * l, HY_WIDTH)


def _softmax_pv(s_list, v_list):
    m = s_list[0].max(axis=-1, keepdims=True)
    for s in s_list[1:]:
        m = jnp.maximum(m, s.max(axis=-1, keepdims=True))
    den = 0.0
    o = 0.0
    for s, v in zip(s_list, v_list):
        e = jnp.exp2(s - m)
        den = den + e.sum(axis=-1, keepdims=True)
        o = o + _mm(e.astype(BF16), v)
    return o / den


def _ctx_attn_kernel(qna_ref, qg_ref, nak_ref, nav_ref, gk_ref, gv_ref, ona_ref, og_ref):
    for hd in range(NA_HEADS):
        sl = slice(hd * HEAD_DIM, (hd + 1) * HEAD_DIM)
        k = nak_ref[0, hd].astype(BF16)
        v = nav_ref[0, hd].astype(BF16)
        ona_ref[:, sl] = _softmax_pv([_nt(qna_ref[:, sl], k)], [v]).astype(BF16)
    for g in range(GQA_KV_HEADS):
        k = gk_ref[0, g].astype(BF16)
        v = gv_ref[0, g].astype(BF16)
        for r in range(GQA_GROUP):
            hd = g * GQA_GROUP + r
            sl = slice(hd * HEAD_DIM, (hd + 1) * HEAD_DIM)
            og_ref[:, sl] = _softmax_pv([_nt(qg_ref[:, sl], k)], [v]).astype(BF16)


def _ctx_attention(qna, qg, nak, nav, gk, gv):
    batch, _, seq, _ = nak.shape
    t = batch * seq
    kv = lambda hh: pl.BlockSpec((1, hh, seq, HEAD_DIM), lambda i: (i, 0, 0, 0))
    row = lambda w: pl.BlockSpec((seq, w), lambda i: (i, 0))
    return pl.pallas_call(
        _ctx_attn_kernel,
        out_shape=[jax.ShapeDtypeStruct((t, NA_W), BF16), jax.ShapeDtypeStruct((t, GQ_W), BF16)],
        grid=(batch,),
        in_specs=[row(NA_W), row(GQ_W), kv(NA_HEADS), kv(NA_HEADS), kv(GQA_KV_HEADS), kv(GQA_KV_HEADS)],
        out_specs=[row(NA_W), row(GQ_W)],
        compiler_params=_cp(1),
        name="ctx_attention",
    )(qna, qg, nak, nav, gk, gv)


NA_Q_ROWS = 2


def _na_tables(rows):
    r_q = NA_Q_ROWS
    kh = min(NA_WIN_H, rows)
    win = min(r_q + kh, rows)
    nblk = rows // r_q
    starts, var_ids, variants, keys = [], [], [], {}
    qr = np.arange(r_q)[:, None]
    kr = np.arange(win)[None, :]
    for j in range(nblk):
        start = int(np.clip(r_q * j - kh // 2, 0, rows - win))
        r = r_q * j + qr
        rs = np.clip(r - kh // 2, 0, rows - kh)
        kabs = start + kr
        row_ok = (kabs >= rs) & (kabs < rs + kh)
        ridx = np.where(row_ok, kabs - r + NA_WIN_H - 1, 0)
        key = ridx.tobytes() + row_ok.tobytes()
        if key not in keys:
            keys[key] = len(variants)
            variants.append((ridx, row_ok))
        starts.append(start)
        var_ids.append(keys[key])
    ridx = np.stack([v[0] for v in variants])
    row_ok = np.stack([v[1] for v in variants])
    qc = np.arange(GRID_W)[:, None]
    kc = np.arange(GRID_W)[None, :]
    cstart = np.clip(qc - NA_WIN_W // 2, 0, GRID_W - NA_WIN_W)
    col_ok = (kc >= cstart) & (kc < cstart + NA_WIN_W)
    return win, np.asarray(starts, np.int32), np.asarray(var_ids, np.int32), ridx, row_ok, col_ok


def _na_bias(rpb, ridx, row_ok, col_ok):
    h, nd, nrel = rpb.shape
    half = NA_WIN_W - 1
    period = 2 * GRID_W
    w = jnp.concatenate([rpb[..., half:], jnp.zeros((h, nd, period - nrel), F32), rpb[..., :half]], axis=-1)
    toep = jnp.tile(w, (1, 1, GRID_W))[..., :GRID_W * (period - 1)]
    toep = toep.reshape(h, nd, GRID_W, period - 1)[..., :GRID_W]
    toep = jnp.where(jnp.asarray(col_ok), toep * LOG2_E, MASK_VALUE)
    nvar, r_q, win = ridx.shape
    blocks = jnp.take(toep, jnp.asarray(ridx.reshape(-1)), axis=1)
    blocks = blocks.reshape(h, nvar, r_q, win, GRID_W, GRID_W)
    blocks = jnp.where(jnp.asarray(row_ok)[None, :, :, :, None, None], blocks, MASK_VALUE)
    return blocks.transpose(0, 1, 2, 4, 3, 5).reshape(h, nvar, r_q * GRID_W, win * GRID_W)


def _na_lat_kernel(win, start_ref, var_ref, q_ref, k_ref, v_ref, kc_ref, vc_ref, bias_ref, o_ref):
    j = pl.program_id(2)
    off = pl.multiple_of(start_ref[j] * GRID_W, GRID_W)
    q = q_ref[...]
    kw = k_ref[pl.ds(off, win * GRID_W), :]
    vw = v_ref[pl.ds(off, win * GRID_W), :]
    s_loc = _nt(q, kw) + bias_ref[0, 0]
    s_ctx = _nt(q, kc_ref[0, 0, 0].astype(BF16))
    o_ref[...] = _softmax_pv([s_loc, s_ctx], [vw, vc_ref[0, 0, 0].astype(BF16)]).astype(BF16)


def _na_latent(q, k, v, kc, vc, layer, rpb, batch, n):
    rows = n // GRID_W
    win, starts, var_ids, ridx, row_ok, col_ok = _na_tables(rows)
    bias = _na_bias(rpb, ridx, row_ok, col_ok)
    r_q = NA_Q_ROWS
    nblk = rows // r_q
    tq = r_q * GRID_W
    wk = win * GRID_W
    past = kc.shape[3]
    grid_spec = pltpu.PrefetchScalarGridSpec(
        num_scalar_prefetch=2,
        grid=(batch, NA_HEADS, nblk),
        in_specs=[
            pl.BlockSpec((tq, HEAD_DIM), lambda b, h, j, st, vr: (b * nblk + j, h)),
            pl.BlockSpec((n, HEAD_DIM), lambda b, h, j, st, vr: (b, h)),
            pl.BlockSpec((n, HEAD_DIM), lambda b, h, j, st, vr: (b, h)),
            pl.BlockSpec((1, 1, 1, past, HEAD_DIM), lambda b, h, j, st, vr: (b, layer, h, 0, 0)),
            pl.BlockSpec((1, 1, 1, past, HEAD_DIM), lambda b, h, j, st, vr: (b, layer, h, 0, 0)),
            pl.BlockSpec((1, 1, tq, wk), lambda b, h, j, st, vr: (h, vr[j], 0, 0)),
        ],
        out_specs=pl.BlockSpec((tq, HEAD_DIM), lambda b, h, j, st, vr: (b * nblk + j, h)),
    )
    return pl.pallas_call(
        functools.partial(_na_lat_kernel, win),
        out_shape=jax.ShapeDtypeStruct((batch * n, NA_W), BF16),
        grid_spec=grid_spec,
        compiler_params=_cp(3),
        name="na_latent",
    )(jnp.asarray(starts), jnp.asarray(var_ids), q, k, v, kc, vc, bias)


GQA_KEY_CHUNK = 512


def _gqa_lat_kernel(q_ref, k_ref, v_ref, kc_ref, vc_ref, o_ref, kall_ref, vall_ref, m_ref, acc_ref):
    tq = q_ref.shape[0]
    n = k_ref.shape[0]
    past = kc_ref.shape[3]
    total = n + past

    @pl.when(pl.program_id(2) == 0)
    def _():
        kall_ref[0:n, :] = k_ref[...]
        kall_ref[n:total, :] = kc_ref[0, 0, 0].astype(BF16)
        vall_ref[0:n, 0:HEAD_DIM] = v_ref[...]
        vall_ref[n:total, 0:HEAD_DIM] = vc_ref[0, 0, 0].astype(BF16)
        vall_ref[:, HEAD_DIM:2 * HEAD_DIM] = jnp.ones((total, HEAD_DIM), BF16)

    q = jnp.concatenate([q_ref[:, r * HEAD_DIM:(r + 1) * HEAD_DIM] for r in range(GQA_GROUP)], axis=0)
    m_ref[...] = jnp.full(m_ref.shape, -jnp.inf, F32)
    acc_ref[...] = jnp.zeros(acc_ref.shape, F32)

    def update(off, size):
        s = _nt(q, kall_ref[pl.ds(off, size), :])
        m_old = m_ref[...]
        m_new = jnp.maximum(m_old, s.max(axis=-1, keepdims=True))
        alpha = jnp.exp2(m_old - m_new)
        e = jnp.exp2(s - jnp.concatenate([m_new] * (size // LANES), axis=1))
        acc_ref[...] = (jnp.concatenate([alpha, alpha], axis=1) * acc_ref[...]
                        + _mm(e.astype(BF16), vall_ref[pl.ds(off, size), :]))
        m_ref[...] = m_new

    tk = min(GQA_KEY_CHUNK, total)
    full = total // tk

    def body(c, carry):
        update(pl.multiple_of(c * tk, tk), tk)
        return carry

    lax.fori_loop(0, full, body, 0)
    if total % tk:
        update(full * tk, total % tk)
    acc = acc_ref[...]
    o = acc[:, 0:HEAD_DIM] / acc[:, HEAD_DIM:2 * HEAD_DIM]
    for r in range(GQA_GROUP):
        o_ref[:, r * HEAD_DIM:(r + 1) * HEAD_DIM] = o[r * tq:(r + 1) * tq].astype(BF16)


def _gqa_latent(q, k, v, kc, vc, layer, batch, n):
    tq = min(256, n)
    nq = n // tq
    past = kc.shape[3]
    gw = GQA_GROUP * HEAD_DIM
    return pl.pallas_call(
        _gqa_lat_kernel,
        out_shape=jax.ShapeDtypeStruct((batch * n, GQ_W), BF16),
        grid=(batch, GQA_KV_HEADS, nq),
        in_specs=[
            pl.BlockSpec((tq, gw), lambda b, g, i: (b * nq + i, g)),
            pl.BlockSpec((n, HEAD_DIM), lambda b, g, i: (b, g)),
            pl.BlockSpec((n, HEAD_DIM), lambda b, g, i: (b, g)),
            pl.BlockSpec((1, 1, 1, past, HEAD_DIM), lambda b, g, i: (b, layer, g, 0, 0)),
            pl.BlockSpec((1, 1, 1, past, HEAD_DIM), lambda b, g, i: (b, layer, g, 0, 0)),
        ],
        out_specs=pl.BlockSpec((tq, gw), lambda b, g, i: (b * nq + i, g)),
        scratch_shapes=[pltpu.VMEM((n + past, HEAD_DIM), BF16), pltpu.VMEM((n + past, 2 * HEAD_DIM), BF16),
                        pltpu.VMEM((GQA_GROUP * tq, LANES), F32),
                        pltpu.VMEM((GQA_GROUP * tq, 2 * HEAD_DIM), F32)],
        compiler_params=_cp(3),
        name="gqa_latent",
    )(q, k, v, kc, vc)


def _outproj_kernel(x_ref, hy_ref, ona_ref, og_ref, mod_ref, g2_ref, wo_ref, rwh_ref, rwl_ref,
                    x1_ref, h2_ref, aff_ref):
    d = x_ref.shape[-1]
    mod = mod_ref[0]
    gate1 = mod[:, 2 * d:3 * d]
    shift2, scale2 = mod[:, 3 * d:4 * d], mod[:, 4 * d:5 * d]
    c1 = HY_WIDTH
    c2 = HY_WIDTH + NA_W
    c3 = c2 + GQ_W
    mixed = (_mm(hy_ref[...], wo_ref[0:c1, :]) + _mm(ona_ref[...], wo_ref[c1:c2, :])
             + _mm(og_ref[...], wo_ref[c2:c3, :]))
    x1 = x_ref[...] + gate1 * mixed
    x1_ref[...] = x1
    ms = jnp.mean(x1 * x1, axis=-1, keepdims=True)
    h2 = (x1 * lax.rsqrt(ms + NORM_EPS) * g2_ref[...]) * (1.0 + scale2) + shift2
    h2h = h2.astype(BF16)
    h2_ref[...] = h2h
    h2l = (h2 - h2h.astype(F32)).astype(BF16)
    logits = _nt(rwh_ref[...], h2h) + _nt(rwh_ref[...], h2l) + _nt(rwl_ref[...], h2h)
    m = logits.max(axis=0, keepdims=True)
    e = jnp.exp(logits - m)
    aff_ref[...] = e / e.sum(axis=0, keepdims=True)


def _out_projection(x2d, hy, ona, og, mod3, mod_base, seq, g2, w_out_bf, rw_hi, rw_lo, latent):
    t, d = x2d.shape
    tm = min(512, seq) if latent else min(512, t)
    tiles_per_batch = max(seq // tm, 1)
    if latent:
        mod_idx = lambda i: (mod_base + i // tiles_per_batch, 0, 0)
    else:
        mod_idx = lambda i: (mod_base, 0, 0)
    row = lambda w: pl.BlockSpec((tm, w), lambda i: (i, 0))
    ne = rw_hi.shape[0]
    return pl.pallas_call(
        _outproj_kernel,
        out_shape=[jax.ShapeDtypeStruct((t, d), F32), jax.ShapeDtypeStruct((t, d), BF16),
                   jax.ShapeDtypeStruct((ne, t), F32)],
        grid=(t // tm,),
        in_specs=[row(d), row(HY_WIDTH), row(NA_W), row(GQ_W),
                  pl.BlockSpec((1, 1, mod3.shape[-1]), mod_idx),
                  pl.BlockSpec((1, d), lambda i: (0, 0)),
                  pl.BlockSpec(w_out_bf.shape, lambda i: (0, 0), pipeline_mode=pl.Buffered(1)),
                  pl.BlockSpec((ne, d), lambda i: (0, 0)),
                  pl.BlockSpec((ne, d), lambda i: (0, 0))],
        out_specs=[row(d), row(d), pl.BlockSpec((ne, tm), lambda i: (0, i))],
        compiler_params=_cp(1),
        name="out_proj_lat" if latent else "out_proj_ctx",
    )(x2d, hy, ona, og, mod3, g2.reshape(1, d), w_out_bf, rw_hi, rw_lo)


ROUTE_CHUNK = 256


def _prefix_count(mask_f, tri):
    ne, n = mask_f.shape
    ch = tri.shape[0]
    pieces = []
    carry = jnp.zeros((ne, 1), F32)
    for c in range(n // ch):
        mk = mask_f[:, c * ch:(c + 1) * ch]
        inc = _mm(mk.astype(BF16), tri)
        pieces.append(inc - mk + carry)
        carry = carry + inc[:, ch - 1:ch]
    return (pieces[0] if len(pieces) == 1 else jnp.concatenate(pieces, axis=1)), carry


def _route_kernel(cap, aff_ref, slot_ref, slott_ref, gslot_ref):
    aff = aff_ref[...]
    ne, n = aff.shape

    def bisect(i, thr_bits):
        cand = thr_bits | (jnp.int32(1) << (30 - i))
        cnt = jnp.sum(jnp.where(aff >= pltpu.bitcast(cand, F32), 1.0, 0.0), axis=1, keepdims=True)
        return jnp.where(cnt >= cap, cand, thr_bits)

    thr = pltpu.bitcast(lax.fori_loop(0, 31, bisect, jnp.zeros((ne, 1), jnp.int32)), F32)
    ch = min(ROUTE_CHUNK, n)
    tri = jnp.where(lax.broadcasted_iota(jnp.int32, (ch, ch), 0) <= lax.broadcasted_iota(jnp.int32, (ch, ch), 1),
                    1.0, 0.0).astype(BF16)
    gt = jnp.where(aff > thr, 1.0, 0.0)
    eq = jnp.where(aff == thr, 1.0, 0.0)
    n_gt = jnp.sum(gt, axis=1, keepdims=True)
    eq_rank, _ = _prefix_count(eq, tri)
    sel = gt + eq * (eq_rank < (cap - n_gt)).astype(F32)
    rank, _ = _prefix_count(sel, tri)
    slot = jnp.where(sel > 0.0, rank, -1.0).astype(jnp.int32)
    slot_ref[...] = slot
    pad = jnp.zeros((LANES - ne, ch), F32)
    for c in range(n // ch):
        blk = jnp.concatenate([slot[:, c * ch:(c + 1) * ch].astype(F32), pad], axis=0)
        slott_ref[c * ch:(c + 1) * ch, :] = blk.T[:, :ne].astype(jnp.int32)
    srow = lax.broadcasted_iota(jnp.int32, (cap, n), 0)
    for e in range(ne):
        hit = slot[e:e + 1, :] == srow
        gslot_ref[e, 0] = jnp.sum(jnp.where(hit, aff[e:e + 1, :], 0.0), axis=1, keepdims=True)


def _route(aff_t, n_sets, n):
    ne, t = aff_t.shape
    cap = EC_CAPACITY_FACTOR * n // N_EXPERTS
    return pl.pallas_call(
        functools.partial(_route_kernel, cap),
        out_shape=[jax.ShapeDtypeStruct((ne, t), jnp.int32), jax.ShapeDtypeStruct((t, ne), jnp.int32),
                   jax.ShapeDtypeStruct((ne, n_sets, cap, 1), F32)],
        grid=(n_sets,),
        in_specs=[pl.BlockSpec((ne, n), lambda b: (0, b))],
        out_specs=[pl.BlockSpec((ne, n), lambda b: (0, b)), pl.BlockSpec((n, ne), lambda b: (b, 0)),
                   pl.BlockSpec((ne, 1, cap, 1), lambda b: (0, b, 0, 0))],
        compiler_params=_cp(1),
        name="route",
    )(aff_t)


GATHER_ROWS = 512


def _gather_kernel(slot_ref, h_ref, o_ref):
    eb, _, cap, _ = o_ref.shape
    n = slot_ref.shape[-1]
    srow = lax.broadcasted_iota(jnp.int32, (cap, n), 0)
    onehot = [jnp.where(slot_ref[e] == srow, 1.0, 0.0).astype(BF16) for e in range(eb)]
    onehot = onehot[0] if eb == 1 else jnp.concatenate(onehot, axis=0)
    xg = _mm(onehot, h_ref[...]).astype(BF16)
    for e in range(eb):
        o_ref[e, 0] = xg[e * cap:(e + 1) * cap]


def _gather(slot, h2, n_sets, n):
    ne, t = slot.shape
    d = h2.shape[1]
    cap = EC_CAPACITY_FACTOR * n // N_EXPERTS
    eb = min(ne, max(1, GATHER_ROWS // cap))
    td = min(d, 512)
    slot3 = slot.reshape(ne, 1, t)
    return pl.pallas_call(
        _gather_kernel,
        out_shape=jax.ShapeDtypeStruct((ne, n_sets, cap, d), BF16),
        grid=(n_sets, d // td, ne // eb),
        in_specs=[pl.BlockSpec((eb, 1, n), lambda b, j, e: (e, 0, b)),
                  pl.BlockSpec((n, td), lambda b, j, e: (b, j))],
        out_specs=pl.BlockSpec((eb, 1, cap, td), lambda b, j, e: (e, b, 0, j)),
        compiler_params=_cp(3),
        name="moe_gather",
    )(slot3, h2)


def _ffn_kernel(x_ref, g_ref, wg_ref, wu_ref, wd_ref, y_ref, acc_ref):
    f = pl.program_id(2)

    @pl.when(f == 0)
    def _():
        acc_ref[...] = jnp.zeros_like(acc_ref)

    x = x_ref[0]
    a = _mm(x, wg_ref[0, 0].astype(BF16))
    u = _mm(x, wu_ref[0, 0].astype(BF16))
    hmid = (a * jax.nn.sigmoid(a) * u).astype(BF16)
    acc_ref[...] += _mm(hmid, wd_ref[0, 0].astype(BF16))

    @pl.when(f == pl.num_programs(2) - 1)
    def _():
        y_ref[0] = (acc_ref[...] * g_ref[0]).astype(BF16)


FFN_ROWS = 1024
FFN_COLS = 512


def _expert_ffn(xg, gslot, w_gate, w_up, w_down, layer):
    ne, m, d = xg.shape
    ff = w_gate.shape[-1]
    tf = min(ff, FFN_COLS)
    tm = min(m, FFN_ROWS)
    return pl.pallas_call(
        _ffn_kernel,
        out_shape=jax.ShapeDtypeStruct((ne, m, d), BF16),
        grid=(ne, m // tm, ff // tf),
        in_specs=[pl.BlockSpec((1, tm, d), lambda e, i, f: (e, i, 0)),
                  pl.BlockSpec((1, tm, 1), lambda e, i, f: (e, i, 0)),
                  pl.BlockSpec((1, 1, d, tf), lambda e, i, f: (layer, e, 0, f)),
                  pl.BlockSpec((1, 1, d, tf), lambda e, i, f: (layer, e, 0, f)),
                  pl.BlockSpec((1, 1, tf, d), lambda e, i, f: (layer, e, f, 0))],
        out_specs=pl.BlockSpec((1, tm, d), lambda e, i, f: (e, i, 0)),
        scratch_shapes=[pltpu.VMEM((tm, d), F32)],
        compiler_params=_cp(3),
        name="moe_ffn",
    )(xg, gslot, w_gate, w_up, w_down)


def _scatter_kernel(x_ref, slott_ref, y_ref, gate2_ref, o_ref):
    td = x_ref.shape[-1]
    ne, _, cap, _ = y_ref.shape
    tn = x_ref.shape[0]
    lane = lax.broadcasted_iota(jnp.int32, (tn, cap), 1)
    st = slott_ref[...]
    moe = jnp.zeros((tn, td), F32)
    for e in range(ne):
        onehot = jnp.where(st[:, e:e + 1] == lane, 1.0, 0.0).astype(BF16)
        moe = moe + _mm(onehot, y_ref[e, 0])
    o_ref[...] = x_ref[...] + gate2_ref[0] * moe


def _scatter_residual(x1, slot_t, y4, mod3, mod_base, n_sets, n, latent):
    t, d = x1.shape
    ne, _, cap, _ = y4.shape
    tn = min(n, 256)
    td = min(d, 512)
    npt = n // tn
    g2_blk = 5 * (d // td)
    if latent:
        mod_idx = lambda b, j, i: (mod_base + b, 0, g2_blk + j)
    else:
        mod_idx = lambda b, j, i: (mod_base, 0, g2_blk + j)
    return pl.pallas_call(
        _scatter_kernel,
        out_shape=jax.ShapeDtypeStruct((t, d), F32),
        grid=(n_sets, d // td, npt),
        in_specs=[pl.BlockSpec((tn, td), lambda b, j, i: (b * npt + i, j)),
                  pl.BlockSpec((tn, ne), lambda b, j, i: (b * npt + i, 0)),
                  pl.BlockSpec((ne, 1, cap, td), lambda b, j, i: (0, b, 0, j)),
                  pl.BlockSpec((1, 1, td), mod_idx)],
        out_specs=pl.BlockSpec((tn, td), lambda b, j, i: (b * npt + i, j)),
        compiler_params=_cp(3),
        name="moe_scatter",
    )(x1, slot_t, y4, mod3)


def _mixer_ctx(x2d, batch, seq, mod3, lp):
    hy, qna, qg, nak, nav, gk, gv = _in_projection(
        x2d, mod3, 0, batch, seq, lp["norm1_g"], lp["w_in_bf"], lp["head_gains"], latent=False)
    y_hy = _hyena(hy.reshape(batch, seq, HY_IN), lp)
    ona, og = _ctx_attention(qna, qg, nak, nav, gk, gv)
    x1, h2, aff = _out_projection(x2d, y_hy, ona, og, mod3, 0, seq, lp["norm2_g"], lp["w_out_bf"],
                                  lp["rw_hi"], lp["rw_lo"], latent=False)
    return x1, h2, aff, (nak, nav, gk, gv)


def _mixer_lat(x2d, batch, seq, mod3, lp, caches, layer):
    hy, qna, kna, vna, qg, kg, vg = _in_projection(
        x2d, mod3, 1, batch, seq, lp["norm1_g"], lp["w_in_bf"], lp["head_gains"], latent=True)
    y_hy = _hyena(hy.reshape(batch, seq, HY_IN), lp)
    na_kc, na_vc, g_kc, g_vc = caches
    ona = _na_latent(qna, kna, vna, na_kc, na_vc, layer, lp["na_rpb"], batch, seq)
    og = _gqa_latent(qg, kg, vg, g_kc, g_vc, layer, batch, seq)
    x1, h2, aff = _out_projection(x2d, y_hy, ona, og, mod3, 1, seq, lp["norm2_g"], lp["w_out_bf"],
                                  lp["rw_hi"], lp["rw_lo"], latent=True)
    return x1, h2, aff


def _moe(parts, mod3, lp):
    outs = []
    for x1, h2, aff, n_sets, n, mod_base, latent in parts:
        slot, slot_t, gslot = _route(aff, n_sets, n)
        xg = _gather(slot, h2, n_sets, n)
        ne, _, cap, d = xg.shape
        y = _expert_ffn(xg.reshape(ne, n_sets * cap, d), gslot.reshape(ne, n_sets * cap, 1),
                        lp["exp_w_gate"], lp["exp_w_up"], lp["exp_w_down"], lp["layer"])
        outs.append(_scatter_residual(x1, slot_t, y.reshape(xg.shape), mod3, mod_base, n_sets, n, latent))
    return outs


def kernel(x_prompt, x_sample, cache_na_k, cache_na_v, cache_gqa_k, cache_gqa_v, c, c_ctx, ada_w, ada_b, norm1_g, norm2_g, w_in, w_out, hy_short_w, hy_short_b, hy_f_w1, hy_f_b1, hy_f_w2, hy_f_b2, hy_f_w3, hy_f_freq, hy_bias, na_q_g, na_k_g, na_rpb, gqa_q_g, gqa_k_g, router_w, exp_w_gate, exp_w_up, exp_w_down):
    batch, seq, d = x_prompt.shape
    dbatch, dseq, _ = x_sample.shape
    depth = ada_w.shape[0]
    rows = 8 * ((1 + dbatch + 7) // 8)
    cvec = jnp.zeros((rows, d), F32).at[0].set(c_ctx).at[1:1 + dbatch].set(c)
    mod_all = _modulation(cvec, ada_w, ada_b)
    yp = x_prompt.reshape(batch * seq, d)
    ys = x_sample.reshape(dbatch * dseq, d)
    rw_t = jnp.swapaxes(router_w, 1, 2)
    rw_hi = rw_t.astype(BF16)
    rw_lo = (rw_t - rw_hi.astype(F32)).astype(BF16)
    new_kv = [[], [], [], []]
    for l in range(depth):
        lp = {
            "norm1_g": norm1_g[l], "norm2_g": norm2_g[l],
            "w_in_bf": w_in[l].astype(BF16), "w_out_bf": w_out[l].astype(BF16),
            "hy_short_w": hy_short_w[l], "hy_short_b": hy_short_b[l],
            "hy_f_w1": hy_f_w1[l], "hy_f_b1": hy_f_b1[l], "hy_f_w2": hy_f_w2[l], "hy_f_b2": hy_f_b2[l],
            "hy_f_w3": hy_f_w3[l], "hy_f_freq": hy_f_freq[l], "hy_bias": hy_bias[l],
            "head_gains": jnp.stack([na_q_g[l], na_k_g[l], gqa_q_g[l], gqa_k_g[l]]),
            "na_rpb": na_rpb[l], "rw_hi": rw_hi[l], "rw_lo": rw_lo[l],
            "layer": l, "exp_w_gate": exp_w_gate, "exp_w_up": exp_w_up, "exp_w_down": exp_w_down,
        }
        mod3 = mod_all[l].reshape(rows, 1, 6 * d)
        xp1, hp2, affp, kv = _mixer_ctx(yp, batch, seq, mod3, lp)
        for dst, src in zip(new_kv, kv):
            dst.append(src)
        xs1, hs2, affs = _mixer_lat(ys, dbatch, dseq, mod3, lp,
                                    (cache_na_k, cache_na_v, cache_gqa_k, cache_gqa_v), l)
        yp, ys = _moe([(xp1, hp2, affp, batch, seq, 0, False), (xs1, hs2, affs, dbatch, dseq, 1, True)],
                      mod3, lp)
    outs = [jnp.stack(v, axis=1) for v in new_kv]
    return (yp.reshape(batch, seq, d), ys.reshape(dbatch, dseq, d), outs[0], outs[1], outs[2], outs[3])
```

```python
import functools
import math

import numpy as np
import jax
import jax.numpy as jnp
from jax import lax
from jax.experimental import pallas as pl
from jax.experimental.pallas import tpu as pltpu

F32 = jnp.float32
BF16 = jnp.bfloat16

HEAD_DIM = 128
GRID_W = 64
HY_WIDTH = 512
HY_ORDER = 2
HY_IN = (HY_ORDER + 1) * HY_WIDTH
HY_POS_EMB = 33
HY_DECAY_TARGET = 1e-2
HY_FAST_PCT = 0.3
HY_SLOW_PCT = 1.5
NA_HEADS = 6
NA_WIN_H = 8
NA_WIN_W = 16
GQA_Q_HEADS = 6
GQA_KV_HEADS = 2
GQA_GROUP = GQA_Q_HEADS // GQA_KV_HEADS
ROPE_THETA = 10000.0
N_EXPERTS = 16
EC_CAPACITY_FACTOR = 2
NORM_EPS = 1e-6
MASK_VALUE = -1e30
NA_W = NA_HEADS * HEAD_DIM
GQ_W = GQA_Q_HEADS * HEAD_DIM
GKV_W = GQA_KV_HEADS * HEAD_DIM

LANES = 128
VMEM_LIMIT_BYTES = 56 * 1024 * 1024
LOG2_E = math.log2(math.e)
DFT_INNER = 128
DIRECT_DFT_MAX_LEN = 512


def _cp(n_axes, vmem=VMEM_LIMIT_BYTES):
    return pltpu.CompilerParams(dimension_semantics=("arbitrary",) * n_axes, vmem_limit_bytes=vmem)


def _nt(a, b):
    return lax.dot_general(a, b, (((1,), (1,)), ((), ())), preferred_element_type=F32)


def _mm(a, b):
    return jnp.dot(a, b, preferred_element_type=F32)


def _mm_hi(a, b):
    return jnp.dot(a, b, preferred_element_type=F32, precision=lax.Precision.HIGHEST)


def _mod_kernel(c_ref, w_ref, b_ref, o_ref):
    c = c_ref[...]
    s = c * jax.nn.sigmoid(c)
    o_ref[0] = _mm(s.astype(BF16), w_ref[0].astype(BF16)) + b_ref[0]


def _modulation(cvec, ada_w, ada_b):
    depth, d, n = ada_w.shape
    rows = cvec.shape[0]
    tn = min(n, 512)
    return pl.pallas_call(
        _mod_kernel,
        out_shape=jax.ShapeDtypeStruct((depth, rows, n), F32),
        grid=(depth, n // tn),
        in_specs=[
            pl.BlockSpec((rows, d), lambda l, j: (0, 0)),
            pl.BlockSpec((1, d, tn), lambda l, j: (l, 0, j)),
            pl.BlockSpec((1, 1, tn), lambda l, j: (l, 0, j)),
        ],
        out_specs=pl.BlockSpec((1, rows, tn), lambda l, j: (l, 0, j)),
        compiler_params=_cp(2),
        name="modulation",
    )(cvec, ada_w, ada_b.reshape(depth, 1, n))


def _head_norm(t, g):
    return t * lax.rsqrt(jnp.mean(t * t, axis=-1, keepdims=True) + NORM_EPS) * g


def _rope(t, cos, sin):
    swapped = jnp.where((lax.broadcasted_iota(jnp.int32, t.shape, 1) % 64) < 32,
                        pltpu.roll(t, HEAD_DIM - 32, 1), pltpu.roll(t, 32, 1))
    return t * cos + swapped * sin


def _inproj_kernel(latent, x_ref, mod_ref, g1_ref, w_ref, hg_ref, *rest):
    if latent:
        cos_ref, sin_ref, hy_ref, qna_ref, kna_ref, vna_ref, qg_ref, kg_ref, vg_ref = rest
    else:
        hy_ref, qna_ref, qg_ref, nak_ref, nav_ref, gk_ref, gv_ref = rest
    d = x_ref.shape[-1]
    x = x_ref[...]
    mod = mod_ref[0]
    shift, scale = mod[:, 0:d], mod[:, d:2 * d]
    ms = jnp.mean(x * x, axis=-1, keepdims=True)
    h = (x * lax.rsqrt(ms + NORM_EPS) * g1_ref[...]) * (1.0 + scale) + shift
    hb = h.astype(BF16)

    def proj(c0, n):
        return _mm(hb, w_ref[:, c0:c0 + n])

    hy_ref[...] = proj(0, HY_IN)
    q_scale = HEAD_DIM ** -0.5 * LOG2_E
    g_naq, g_nak = hg_ref[0:1, :], hg_ref[1:2, :]
    g_gq, g_gk = hg_ref[2:3, :], hg_ref[3:4, :]
    c = HY_IN
    z_naq = proj(c, NA_W)
    z_nak = proj(c + NA_W, NA_W)
    z_nav = proj(c + 2 * NA_W, NA_W)
    c += 3 * NA_W
    z_gq = proj(c, GQ_W)
    z_gk = proj(c + GQ_W, GKV_W)
    z_gv = proj(c + GQ_W + GKV_W, GKV_W)
    if latent:
        cos, sin = cos_ref[...], sin_ref[...]
    seq = None if latent else nak_ref.shape[2]
    nb = None if latent else nak_ref.shape[0]
    for hd in range(NA_HEADS):
        sl = slice(hd * HEAD_DIM, (hd + 1) * HEAD_DIM)
        qna_ref[:, sl] = (_head_norm(z_naq[:, sl], g_naq) * q_scale).astype(BF16)
        k = _head_norm(z_nak[:, sl], g_nak)
        v = z_nav[:, sl]
        if latent:
            kna_ref[:, sl] = k.astype(BF16)
            vna_ref[:, sl] = v.astype(BF16)
        else:
            for b in range(nb):
                nak_ref[b, hd] = k[b * seq:(b + 1) * seq]
                nav_ref[b, hd] = v[b * seq:(b + 1) * seq]
    for hd in range(GQA_Q_HEADS):
        sl = slice(hd * HEAD_DIM, (hd + 1) * HEAD_DIM)
        q = _head_norm(z_gq[:, sl], g_gq)
        if latent:
            q = _rope(q, cos, sin)
        qg_ref[:, sl] = (q * q_scale).astype(BF16)
    for hd in range(GQA_KV_HEADS):
        sl = slice(hd * HEAD_DIM, (hd + 1) * HEAD_DIM)
        k = _head_norm(z_gk[:, sl], g_gk)
        v = z_gv[:, sl]
        if latent:
            kg_ref[:, sl] = _rope(k, cos, sin).astype(BF16)
            vg_ref[:, sl] = v.astype(BF16)
        else:
            for b in range(nb):
                gk_ref[b, hd] = k[b * seq:(b + 1) * seq]
                gv_ref[b, hd] = v[b * seq:(b + 1) * seq]


def _rope_tables(n):
    pos = np.arange(n)
    row = (pos // GRID_W).astype(np.float64)
    col = (pos % GRID_W).astype(np.float64)
    quarter = HEAD_DIM // 4
    inv = ROPE_THETA ** (-np.arange(quarter, dtype=np.float64) / quarter)
    ang_r = row[:, None] * inv[None, :]
    ang_c = col[:, None] * inv[None, :]
    cos = np.concatenate([np.cos(ang_r), np.cos(ang_r), np.cos(ang_c), np.cos(ang_c)], axis=-1)
    sin = np.concatenate([-np.sin(ang_r), np.sin(ang_r), -np.sin(ang_c), np.sin(ang_c)], axis=-1)
    return jnp.asarray(cos, F32), jnp.asarray(sin, F32)


def _in_projection(x2d, mod3, mod_base, batch, seq, g1, w_in_bf, head_gains, latent):
    t, d = x2d.shape
    tm = min(512, seq) if latent else min(512, t)
    if not latent:
        tm = max(tm, seq)
    nt_ = t // tm
    tiles_per_batch = seq // tm if latent else None
    wide = w_in_bf.shape[1]
    if latent:
        mod_idx = lambda i: (mod_base + i // tiles_per_batch, 0, 0)
    else:
        mod_idx = lambda i: (mod_base, 0, 0)
    in_specs = [
        pl.BlockSpec((tm, d), lambda i: (i, 0)),
        pl.BlockSpec((1, 1, mod3.shape[-1]), mod_idx),
        pl.BlockSpec((1, d), lambda i: (0, 0)),
        pl.BlockSpec((d, wide), lambda i: (0, 0), pipeline_mode=pl.Buffered(1)),
        pl.BlockSpec((4, HEAD_DIM), lambda i: (0, 0)),
    ]
    args = [x2d, mod3, g1.reshape(1, d), w_in_bf, head_gains]
    row_spec = lambda w: pl.BlockSpec((tm, w), lambda i: (i, 0))
    if latent:
        cos, sin = _rope_tables(seq)
        in_specs += [pl.BlockSpec((tm, HEAD_DIM), lambda i: (i % tiles_per_batch, 0))] * 2
        args += [cos, sin]
        out_shape = [
            jax.ShapeDtypeStruct((t, HY_IN), F32),
            jax.ShapeDtypeStruct((t, NA_W), BF16), jax.ShapeDtypeStruct((t, NA_W), BF16),
            jax.ShapeDtypeStruct((t, NA_W), BF16), jax.ShapeDtypeStruct((t, GQ_W), BF16),
            jax.ShapeDtypeStruct((t, GKV_W), BF16), jax.ShapeDtypeStruct((t, GKV_W), BF16),
        ]
        out_specs = [row_spec(HY_IN), row_spec(NA_W), row_spec(NA_W), row_spec(NA_W),
                     row_spec(GQ_W), row_spec(GKV_W), row_spec(GKV_W)]
    else:
        nb = tm // seq
        kv_spec = lambda hh: pl.BlockSpec((nb, hh, seq, HEAD_DIM), lambda i: (i, 0, 0, 0))
        out_shape = [
            jax.ShapeDtypeStruct((t, HY_IN), F32),
            jax.ShapeDtypeStruct((t, NA_W), BF16), jax.ShapeDtypeStruct((t, GQ_W), BF16),
            jax.ShapeDtypeStruct((batch, NA_HEADS, seq, HEAD_DIM), F32),
            jax.ShapeDtypeStruct((batch, NA_HEADS, seq, HEAD_DIM), F32),
            jax.ShapeDtypeStruct((batch, GQA_KV_HEADS, seq, HEAD_DIM), F32),
            jax.ShapeDtypeStruct((batch, GQA_KV_HEADS, seq, HEAD_DIM), F32),
        ]
        out_specs = [row_spec(HY_IN), row_spec(NA_W), row_spec(GQ_W),
                     kv_spec(NA_HEADS), kv_spec(NA_HEADS), kv_spec(GQA_KV_HEADS), kv_spec(GQA_KV_HEADS)]
    return pl.pallas_call(
        functools.partial(_inproj_kernel, latent),
        out_shape=out_shape,
        grid=(nt_,),
        in_specs=in_specs,
        out_specs=out_specs,
        compiler_params=_cp(1),
        name="in_proj_lat" if latent else "in_proj_ctx",
    )(*args)


def _shortconv_kernel(x_ref, w_ref, b_ref, o_ref):
    o_ref[0, 0] = _short_conv_rows(x_ref[0], w_ref, b_ref)


def _short_conv(hy3, sw, sb):
    b, l, _ = hy3.shape
    tc = 256
    per = HY_WIDTH // tc
    return pl.pallas_call(
        _shortconv_kernel,
        out_shape=jax.ShapeDtypeStruct((HY_ORDER + 1, b, l, HY_WIDTH), F32),
        grid=(b, HY_IN // tc),
        in_specs=[
            pl.BlockSpec((1, l, tc), lambda i, q: (i, 0, q)),
            pl.BlockSpec((3, tc), lambda i, q: (0, q)),
            pl.BlockSpec((1, tc), lambda i, q: (0, q)),
        ],
        out_specs=pl.BlockSpec((1, 1, l, tc), lambda i, q: (q // per, i, 0, q % per)),
        compiler_params=_cp(2),
        name="hy_short_conv",
    )(hy3, sw, sb.reshape(1, HY_IN))


def _filter_kernel(z_ref, w1_ref, b1_ref, w2_ref, b2_ref, w3_ref, fr_ref, dl_ref, f_ref, n_ref):
    i = pl.program_id(0)
    z = z_ref[...]
    sf = fr_ref[...]
    hid = jnp.sin(sf * (_mm_hi(z, w1_ref[...]) + b1_ref[...]))
    hid = jnp.sin(sf * (_mm_hi(hid, w2_ref[...]) + b2_ref[...]))
    filt = _mm_hi(hid, w3_ref[...])
    t = z[:, 0:1]
    decay = jnp.exp(-t * dl_ref[...])
    decay = jnp.concatenate([decay] * (2 * HY_ORDER), axis=-1)
    filt = filt * decay
    half = HY_ORDER * HY_WIDTH
    rows = lax.broadcasted_iota(jnp.int32, filt.shape, 0) + i * filt.shape[0]
    cols = lax.broadcasted_iota(jnp.int32, filt.shape, 1)
    filt = jnp.where((rows == 0) & (cols >= half), 0.0, filt)
    f_ref[...] = filt
    part = jnp.sum(jnp.abs(filt), axis=0, keepdims=True)

    @pl.when(i == 0)
    def _():
        n_ref[...] = jnp.zeros_like(n_ref)

    n_ref[...] += part[:, :half] + part[:, half:]


def _filter_embedding(l):
    t = np.linspace(0.0, 1.0, l, dtype=np.float64)[:, None]
    bands = (HY_POS_EMB - 1) // 2
    fr = np.linspace(1e-4, bands - 1, bands, dtype=np.float64)[None, :]
    w = 2.0 * math.pi * np.arange(l, dtype=np.float64)[:, None] / l
    z = np.concatenate([t, np.cos(fr * w), -np.sin(fr * w)], axis=-1)
    zp = np.zeros((l, LANES), np.float32)
    zp[:, :HY_POS_EMB] = z
    min_d = abs(math.log(HY_DECAY_TARGET) / HY_SLOW_PCT)
    max_d = abs(math.log(HY_DECAY_TARGET) / HY_FAST_PCT)
    deltas = np.linspace(min_d, max_d, HY_WIDTH, dtype=np.float64)[None, :]
    return jnp.asarray(zp, F32), jnp.asarray(deltas, F32)


def _hyena_filters(l, w1, b1, w2, b2, w3, freq):
    zp, deltas = _filter_embedding(l)
    hid = w1.shape[1]
    w1p = jnp.zeros((LANES, hid), F32).at[:HY_POS_EMB].set(w1)
    tl = min(l, 512)
    wide = w3.shape[1]
    full = lambda a: pl.BlockSpec(a.shape, lambda i: (0,) * a.ndim)
    b1r, b2r, frr = b1.reshape(1, hid), b2.reshape(1, hid), freq.reshape(1, hid)
    return pl.pallas_call(
        _filter_kernel,
        out_shape=[jax.ShapeDtypeStruct((l, wide), F32),
                   jax.ShapeDtypeStruct((1, HY_ORDER * HY_WIDTH), F32)],
        grid=(l // tl,),
        in_specs=[pl.BlockSpec((tl, LANES), lambda i: (i, 0)), full(w1p), full(b1r), full(w2), full(b2r),
                  full(w3), full(frr), full(deltas)],
        out_specs=[pl.BlockSpec((tl, wide), lambda i: (i, 0)),
                   pl.BlockSpec((1, HY_ORDER * HY_WIDTH), lambda i: (0, 0))],
        compiler_params=_cp(1),
        name="hy_filters",
    )(zp, w1p, b1r, w2, b2r, w3, frr, deltas)


def _bf16_const(a):
    return jnp.asarray(a, F32).astype(BF16)


def _dft_plan(l):
    n = 2 * l
    n2_len = 1 if l <= DIRECT_DFT_MAX_LEN else DFT_INNER
    n1_len = n // n2_len
    k1 = np.arange(n1_len, dtype=np.float64)[:, None]
    n1 = np.arange(n1_len // 2, dtype=np.float64)[None, :]
    ang = 2.0 * np.pi * k1 * n1 / n1_len
    plan = {
        "n1": n1_len, "n2": n2_len,
        "fs": _bf16_const(np.concatenate([np.cos(ang), -np.sin(ang)], axis=0)),
        "gr": _bf16_const(np.cos(ang).T / n),
        "gi": _bf16_const(-np.sin(ang).T / n),
    }
    if n2_len > 1:
        k2 = np.arange(n2_len, dtype=np.float64)
        a2 = 2.0 * np.pi * np.outer(k2, k2) / n2_len
        fr, fi = np.cos(a2), -np.sin(a2)
        plan["d2"] = _bf16_const(np.block([[fr, -fi], [fi, fr]]))
        plan["d2c"] = _bf16_const(np.block([[fr, fi], [-fi, fr]]))
        at = 2.0 * np.pi * np.arange(n1_len, dtype=np.float64)[:, None] * k2[None, :] / n
        plan["twr"] = jnp.asarray(np.repeat(np.cos(at)[:, :, None], LANES, axis=2), F32)
        plan["twi"] = jnp.asarray(np.repeat(-np.sin(at)[:, :, None], LANES, axis=2), F32)
    return plan


def _dft1_kernel(z_ref, fs_ref, ar_ref, ai_ref):
    a = _mm(fs_ref[...], z_ref[0, 0].astype(BF16))
    n1 = ar_ref.shape[1]
    ar_ref[0] = a[:n1].astype(BF16)
    ai_ref[0] = a[n1:].astype(BF16)


def _dft_stage1(z4, part, plan):
    _, b, kk, nc = z4.shape
    n1 = plan["n1"]
    tn = min(nc, 4096)
    return pl.pallas_call(
        _dft1_kernel,
        out_shape=[jax.ShapeDtypeStruct((b, n1, nc), BF16)] * 2,
        grid=(b, nc // tn),
        in_specs=[pl.BlockSpec((1, 1, kk, tn), lambda i, j: (part, i, 0, j)),
                  pl.BlockSpec((2 * n1, kk), lambda i, j: (0, 0))],
        out_specs=[pl.BlockSpec((1, n1, tn), lambda i, j: (i, 0, j))] * 2,
        compiler_params=_cp(2),
        name="hy_dft_outer",
    )(z4, plan["fs"])


def _twiddle_inner_dft(ar, ai, twr, twi, d2):
    c = ar.shape[-1]
    twr = jnp.concatenate([twr] * (c // LANES), axis=-1)
    twi = jnp.concatenate([twi] * (c // LANES), axis=-1)
    zr = ar * twr - ai * twi
    zi = ar * twi + ai * twr
    x = _mm(d2, jnp.concatenate([zr, zi], axis=0).astype(BF16))
    n2 = ar.shape[0]
    return x[:n2], x[n2:], twr, twi


def _spec_filter_kernel(two_stage, kb, ar_ref, ai_ref, nrm_ref, *rest):
    if two_stage:
        twr_ref, twi_ref, d2_ref, kr_ref, ki_ref = rest
    else:
        kr_ref, ki_ref = rest
    half = HY_ORDER * HY_WIDTH
    inv = 1.0 / nrm_ref[...]

    def combine(xr, xi):
        return (xr[:, :half] + xr[:, half:]) * inv, (xi[:, :half] - xi[:, half:]) * inv

    if two_stage:
        for j in range(kb):
            xr, xi, _, _ = _twiddle_inner_dft(ar_ref[0, j].astype(F32), ai_ref[0, j].astype(F32),
                                              twr_ref[j], twi_ref[j], d2_ref[...])
            kr, ki = combine(xr, xi)
            for o in range(HY_ORDER):
                kr_ref[o, j] = kr[:, o * HY_WIDTH:(o + 1) * HY_WIDTH]
                ki_ref[o, j] = ki[:, o * HY_WIDTH:(o + 1) * HY_WIDTH]
    else:
        kr, ki = combine(ar_ref[0].astype(F32), ai_ref[0].astype(F32))
        for o in range(HY_ORDER):
            kr_ref[o] = kr[:, o * HY_WIDTH:(o + 1) * HY_WIDTH]
            ki_ref[o] = ki[:, o * HY_WIDTH:(o + 1) * HY_WIDTH]


def _filter_spectrum(filt, norm, plan):
    l, wide = filt.shape
    n1, n2 = plan["n1"], plan["n2"]
    ar, ai = _dft_stage1(filt.reshape(1, 1, n1 // 2, n2 * wide), 0, plan)
    two_stage = n2 > 1
    if two_stage:
        kb = 8
        ar = ar.reshape(1, n1, n2, wide)
        ai = ai.reshape(1, n1, n2, wide)
        a_spec = pl.BlockSpec((1, kb, n2, wide), lambda i: (0, i, 0, 0))
        in_specs = [a_spec, a_spec, pl.BlockSpec(norm.shape, lambda i: (0, 0)),
                    pl.BlockSpec((kb, n2, LANES), lambda i: (i, 0, 0)),
                    pl.BlockSpec((kb, n2, LANES), lambda i: (i, 0, 0)),
                    pl.BlockSpec((2 * n2, 2 * n2), lambda i: (0, 0))]
        args = [ar, ai, norm, plan["twr"], plan["twi"], plan["d2"]]
        out_shape = [jax.ShapeDtypeStruct((HY_ORDER, n1, n2, HY_WIDTH), F32)] * 2
        out_specs = [pl.BlockSpec((HY_ORDER, kb, n2, HY_WIDTH), lambda i: (0, i, 0, 0))] * 2
    else:
        kb = min(n1, 256)
        a_spec = pl.BlockSpec((1, kb, wide), lambda i: (0, i, 0))
        in_specs = [a_spec, a_spec, pl.BlockSpec(norm.shape, lambda i: (0, 0))]
        args = [ar, ai, norm]
        out_shape = [jax.ShapeDtypeStruct((HY_ORDER, n1, HY_WIDTH), F32)] * 2
        out_specs = [pl.BlockSpec((HY_ORDER, kb, HY_WIDTH), lambda i: (0, i, 0))] * 2
    return pl.pallas_call(
        functools.partial(_spec_filter_kernel, two_stage, kb),
        out_shape=out_shape,
        grid=(n1 // kb,),
        in_specs=in_specs,
        out_specs=out_specs,
        compiler_params=_cp(1),
        name="hy_filter_spectrum",
    )(*args)


def _spec_mul_kernel(two_stage, kb, ar_ref, ai_ref, kr_ref, ki_ref, *rest):
    if two_stage:
        twr_ref, twi_ref, d2_ref, d2c_ref, br_ref, bi_ref = rest
        for j in range(kb):
            xr, xi, twr, twi = _twiddle_inner_dft(ar_ref[0, j].astype(F32), ai_ref[0, j].astype(F32),
                                                  twr_ref[j], twi_ref[j], d2_ref[...])
            kr, ki = kr_ref[0, j], ki_ref[0, j]
            yr = xr * kr - xi * ki
            yi = xr * ki + xi * kr
            bm = _mm(d2c_ref[...], jnp.concatenate([yr, yi], axis=0).astype(BF16))
            n2 = xr.shape[0]
            br, bi = bm[:n2], bm[n2:]
            br_ref[0, j] = (br * twr + bi * twi).astype(BF16)
            bi_ref[0, j] = (bi * twr - br * twi).astype(BF16)
    else:
        br_ref, bi_ref = rest
        xr, xi = ar_ref[0].astype(F32), ai_ref[0].astype(F32)
        kr, ki = kr_ref[0], ki_ref[0]
        br_ref[0] = (xr * kr - xi * ki).astype(BF16)
        bi_ref[0] = (xr * ki + xi * kr).astype(BF16)


def _spectrum_multiply(ar, ai, kr, ki, order, plan):
    b, n1, nc = ar.shape
    n2 = plan["n2"]
    c = nc // n2
    two_stage = n2 > 1
    if two_stage:
        kb = 8
        ar = ar.reshape(b, n1, n2, c)
        ai = ai.reshape(b, n1, n2, c)
        a_spec = pl.BlockSpec((1, kb, n2, c), lambda i, j: (i, j, 0, 0))
        k_spec = pl.BlockSpec((1, kb, n2, c), lambda i, j: (order, j, 0, 0))
        tw_spec = pl.BlockSpec((kb, n2, LANES), lambda i, j: (j, 0, 0))
        d_spec = pl.BlockSpec((2 * n2, 2 * n2), lambda i, j: (0, 0))
        in_specs = [a_spec, a_spec, k_spec, k_spec, tw_spec, tw_spec, d_spec, d_spec]
        args = [ar, ai, kr, ki, plan["twr"], plan["twi"], plan["d2"], plan["d2c"]]
        out_shape = [jax.ShapeDtypeStruct((b, n1, n2, c), BF16)] * 2
        out_specs = [a_spec, a_spec]
    else:
        kb = min(n1, 256)
        a_spec = pl.BlockSpec((1, kb, c), lambda i, j: (i, j, 0))
        k_spec = pl.BlockSpec((1, kb, c), lambda i, j: (order, j, 0))
        in_specs = [a_spec, a_spec, k_spec, k_spec]
        args = [ar, ai, kr, ki]
        out_shape = [jax.ShapeDtypeStruct((b, n1, c), BF16)] * 2
        out_specs = [a_spec, a_spec]
    br, bi = pl.pallas_call(
        functools.partial(_spec_mul_kernel, two_stage, kb),
        out_shape=out_shape,
        grid=(b, n1 // kb),
        in_specs=in_specs,
        out_specs=out_specs,
        compiler_params=_cp(2),
        name="hy_spectrum_mul",
    )(*args)
    return br.reshape(b, n1, nc), bi.reshape(b, n1, nc)


def _idft1_kernel(br_ref, bi_ref, gr_ref, gi_ref, z_ref, x_ref, bias_ref, o_ref):
    conv = _mm(gr_ref[...], br_ref[0]) + _mm(gi_ref[...], bi_ref[0])
    z = z_ref[0, 0]
    o_ref[0, 0] = (x_ref[0, 0] * (conv + z * bias_ref[...])).astype(o_ref.dtype)


def _idft_gate(br, bi, plan, z4, z_part, u4, x_part, bias_row, out_dtype):
    b, n1, nc = br.shape
    kk = n1 // 2
    tn = min(nc, 4096)
    return pl.pallas_call(
        _idft1_kernel,
        out_shape=jax.ShapeDtypeStruct((1, b, kk, nc), out_dtype),
        grid=(b, nc // tn),
        in_specs=[pl.BlockSpec((1, n1, tn), lambda i, j: (i, 0, j)),
                  pl.BlockSpec((1, n1, tn), lambda i, j: (i, 0, j)),
                  pl.BlockSpec((kk, n1), lambda i, j: (0, 0)),
                  pl.BlockSpec((kk, n1), lambda i, j: (0, 0)),
                  pl.BlockSpec((1, 1, kk, tn), lambda i, j: (z_part, i, 0, j)),
                  pl.BlockSpec((1, 1, kk, tn), lambda i, j: (x_part, i, 0, j)),
                  pl.BlockSpec((1, tn), lambda i, j: (0, j))],
        out_specs=pl.BlockSpec((1, 1, kk, tn), lambda i, j: (0, i, 0, j)),
        compiler_params=_cp(2),
        name="hy_idft_gate",
    )(br, bi, plan["gr"], plan["gi"], z4, u4, bias_row)


def _short_conv_rows(x, w_ref, b_ref):
    n = x.shape[0]
    rows = lax.broadcasted_iota(jnp.int32, x.shape, 0)
    prev = jnp.where(rows == 0, 0.0, pltpu.roll(x, 1, 0))
    nxt = jnp.where(rows == n - 1, 0.0, pltpu.roll(x, n - 1, 0))
    return prev * w_ref[0:1, :] + x * w_ref[1:2, :] + nxt * w_ref[2:3, :] + b_ref[...]


def _hyena_direct_kernel(hy_ref, sw_ref, sb_ref, fs_ref, gr_ref, gi_ref, kr_ref, ki_ref, hb_ref, o_ref):
    u = _short_conv_rows(hy_ref[0], sw_ref, sb_ref)
    nfreq = kr_ref.shape[1]
    z = u[:, 0:HY_WIDTH]
    for o in range(HY_ORDER):
        a = _mm(fs_ref[...], z.astype(BF16))
        ar, ai = a[:nfreq], a[nfreq:]
        kr, ki = kr_ref[o], ki_ref[o]
        br = (ar * kr - ai * ki).astype(BF16)
        bi = (ar * ki + ai * kr).astype(BF16)
        conv = _mm(gr_ref[...], br) + _mm(gi_ref[...], bi)
        z = u[:, (o + 1) * HY_WIDTH:(o + 2) * HY_WIDTH] * (conv + z * hb_ref[o:o + 1, :])
    o_ref[...] = z.astype(o_ref.dtype)


def _hyena_direct(hy3, lp, plan, kr, ki):
    b, l, wide = hy3.shape
    n1 = plan["n1"]
    full = lambda a: pl.BlockSpec(a.shape, lambda i: (0,) * a.ndim)
    sb = lp["hy_short_b"].reshape(1, wide)
    args = [lp["hy_short_w"], sb, plan["fs"], plan["gr"], plan["gi"], kr, ki, lp["hy_bias"]]
    return pl.pallas_call(
        _hyena_direct_kernel,
        out_shape=jax.ShapeDtypeStruct((b * l, HY_WIDTH), BF16),
        grid=(b,),
        in_specs=[pl.BlockSpec((1, l, wide), lambda i: (i, 0, 0))] + [full(a) for a in args],
        out_specs=pl.BlockSpec((l, HY_WIDTH), lambda i: (i, 0)),
        compiler_params=_cp(1),
        name="hy_direct",
    )(hy3, *args)


def _hyena(hy3, lp):
    b, l, _ = hy3.shape
    plan = _dft_plan(l)
    n1, n2 = plan["n1"], plan["n2"]
    filt, norm = _hyena_filters(l, lp["hy_f_w1"], lp["hy_f_b1"], lp["hy_f_w2"], lp["hy_f_b2"],
                                lp["hy_f_w3"], lp["hy_f_freq"])
    kr, ki = _filter_spectrum(filt, norm, plan)
    if n2 == 1:
        return _hyena_direct(hy3, lp, plan, kr, ki)
    u = _short_conv(hy3, lp["hy_short_w"], lp["hy_short_b"])
    u4 = u.reshape(HY_ORDER + 1, b, n1 // 2, n2 * HY_WIDTH)
    z4, z_part = u4, 0
    for o in range(HY_ORDER):
        ar, ai = _dft_stage1(z4, z_part, plan)
        br, bi = _spectrum_multiply(ar, ai, kr, ki, o, plan)
        bias_row = jnp.tile(lp["hy_bias"][o].reshape(1, HY_WIDTH), (1, n2))
        last = o == HY_ORDER - 1
        z4 = _idft_gate(br, bi, plan, z4, z_part, u4, o + 1, bias_row, BF16 if last else F32)
        z_part = 0
    return z4.reshape(b * l, HY_WIDTH)


def _softmax_pv(s_list, v_list):
    m = s_list[0].max(axis=-1, keepdims=True)
    for s in s_list[1:]:
        m = jnp.maximum(m, s.max(axis=-1, keepdims=True))
    den = 0.0
    o = 0.0
    for s, v in zip(s_list, v_list):
        e = jnp.exp2(s - m)
        den = den + e.sum(axis=-1, keepdims=True)
        o = o + _mm(e.astype(BF16), v)
    return o / den


def _ctx_attn_kernel(qna_ref, qg_ref, nak_ref, nav_ref, gk_ref, gv_ref, ona_ref, og_ref):
    for hd in range(NA_HEADS):
        sl = slice(hd * HEAD_DIM, (hd + 1) * HEAD_DIM)
        k = nak_ref[0, hd].astype(BF16)
        v = nav_ref[0, hd].astype(BF16)
        ona_ref[:, sl] = _softmax_pv([_nt(qna_ref[:, sl], k)], [v]).astype(BF16)
    for g in range(GQA_KV_HEADS):
        k = gk_ref[0, g].astype(BF16)
        v = gv_ref[0, g].astype(BF16)
        for r in range(GQA_GROUP):
            hd = g * GQA_GROUP + r
            sl = slice(hd * HEAD_DIM, (hd + 1) * HEAD_DIM)
            og_ref[:, sl] = _softmax_pv([_nt(qg_ref[:, sl], k)], [v]).astype(BF16)


def _ctx_attention(qna, qg, nak, nav, gk, gv):
    batch, _, seq, _ = nak.shape
    t = batch * seq
    kv = lambda hh: pl.BlockSpec((1, hh, seq, HEAD_DIM), lambda i: (i, 0, 0, 0))
    row = lambda w: pl.BlockSpec((seq, w), lambda i: (i, 0))
    return pl.pallas_call(
        _ctx_attn_kernel,
        out_shape=[jax.ShapeDtypeStruct((t, NA_W), BF16), jax.ShapeDtypeStruct((t, GQ_W), BF16)],
        grid=(batch,),
        in_specs=[row(NA_W), row(GQ_W), kv(NA_HEADS), kv(NA_HEADS), kv(GQA_KV_HEADS), kv(GQA_KV_HEADS)],
        out_specs=[row(NA_W), row(GQ_W)],
        compiler_params=_cp(1),
        name="ctx_attention",
    )(qna, qg, nak, nav, gk, gv)


NA_Q_ROWS = 2


def _na_tables(rows):
    r_q = NA_Q_ROWS
    kh = min(NA_WIN_H, rows)
    win = min(r_q + kh, rows)
    nblk = rows // r_q
    starts, var_ids, variants, keys = [], [], [], {}
    qr = np.arange(r_q)[:, None]
    kr = np.arange(win)[None, :]
    for j in range(nblk):
        start = int(np.clip(r_q * j - kh // 2, 0, rows - win))
        r = r_q * j + qr
        rs = np.clip(r - kh // 2, 0, rows - kh)
        kabs = start + kr
        row_ok = (kabs >= rs) & (kabs < rs + kh)
        ridx = np.where(row_ok, kabs - r + NA_WIN_H - 1, 0)
        key = ridx.tobytes() + row_ok.tobytes()
        if key not in keys:
            keys[key] = len(variants)
            variants.append((ridx, row_ok))
        starts.append(start)
        var_ids.append(keys[key])
    ridx = np.stack([v[0] for v in variants])
    row_ok = np.stack([v[1] for v in variants])
    qc = np.arange(GRID_W)[:, None]
    kc = np.arange(GRID_W)[None, :]
    cstart = np.clip(qc - NA_WIN_W // 2, 0, GRID_W - NA_WIN_W)
    col_ok = (kc >= cstart) & (kc < cstart + NA_WIN_W)
    return win, np.asarray(starts, np.int32), np.asarray(var_ids, np.int32), ridx, row_ok, col_ok


def _na_bias(rpb, ridx, row_ok, col_ok):
    h, nd, nrel = rpb.shape
    half = NA_WIN_W - 1
    period = 2 * GRID_W
    w = jnp.concatenate([rpb[..., half:], jnp.zeros((h, nd, period - nrel), F32), rpb[..., :half]], axis=-1)
    toep = jnp.tile(w, (1, 1, GRID_W))[..., :GRID_W * (period - 1)]
    toep = toep.reshape(h, nd, GRID_W, period - 1)[..., :GRID_W]
    toep = jnp.where(jnp.asarray(col_ok), toep * LOG2_E, MASK_VALUE)
    nvar, r_q, win = ridx.shape
    blocks = jnp.take(toep, jnp.asarray(ridx.reshape(-1)), axis=1)
    blocks = blocks.reshape(h, nvar, r_q, win, GRID_W, GRID_W)
    blocks = jnp.where(jnp.asarray(row_ok)[None, :, :, :, None, None], blocks, MASK_VALUE)
    return blocks.transpose(0, 1, 2, 4, 3, 5).reshape(h, nvar, r_q * GRID_W, win * GRID_W)


def _na_lat_kernel(win, start_ref, var_ref, q_ref, k_ref, v_ref, kc_ref, vc_ref, bias_ref, o_ref):
    j = pl.program_id(1)
    off = pl.multiple_of(start_ref[j] * GRID_W, GRID_W)
    for hd in range(NA_HEADS):
        sl = slice(hd * HEAD_DIM, (hd + 1) * HEAD_DIM)
        q = q_ref[:, sl]
        kw = k_ref[pl.ds(off, win * GRID_W), sl]
        vw = v_ref[pl.ds(off, win * GRID_W), sl]
        s_loc = _nt(q, kw) + bias_ref[hd, 0]
        s_ctx = _nt(q, kc_ref[0, 0, hd].astype(BF16))
        o_ref[:, sl] = _softmax_pv([s_loc, s_ctx], [vw, vc_ref[0, 0, hd].astype(BF16)]).astype(BF16)


def _na_latent(q, k, v, kc, vc, layer, rpb, batch, n):
    rows = n // GRID_W
    win, starts, var_ids, ridx, row_ok, col_ok = _na_tables(rows)
    bias = _na_bias(rpb, ridx, row_ok, col_ok)
    r_q = NA_Q_ROWS
    nblk = rows // r_q
    tq = r_q * GRID_W
    wk = win * GRID_W
    past = kc.shape[3]
    grid_spec = pltpu.PrefetchScalarGridSpec(
        num_scalar_prefetch=2,
        grid=(batch, nblk),
        in_specs=[
            pl.BlockSpec((tq, NA_W), lambda b, j, st, vr: (b * nblk + j, 0)),
            pl.BlockSpec((n, NA_W), lambda b, j, st, vr: (b, 0)),
            pl.BlockSpec((n, NA_W), lambda b, j, st, vr: (b, 0)),
            pl.BlockSpec((1, 1, NA_HEADS, past, HEAD_DIM), lambda b, j, st, vr: (b, layer, 0, 0, 0)),
            pl.BlockSpec((1, 1, NA_HEADS, past, HEAD_DIM), lambda b, j, st, vr: (b, layer, 0, 0, 0)),
            pl.BlockSpec((NA_HEADS, 1, tq, wk), lambda b, j, st, vr: (0, vr[j], 0, 0)),
        ],
        out_specs=pl.BlockSpec((tq, NA_W), lambda b, j, st, vr: (b * nblk + j, 0)),
    )
    return pl.pallas_call(
        functools.partial(_na_lat_kernel, win),
        out_shape=jax.ShapeDtypeStruct((batch * n, NA_W), BF16),
        grid_spec=grid_spec,
        compiler_params=_cp(2),
        name="na_latent",
    )(jnp.asarray(starts), jnp.asarray(var_ids), q, k, v, kc, vc, bias)


GQA_KEY_CHUNK = 512


def _gqa_lat_kernel(q_ref, k_ref, v_ref, kc_ref, vc_ref, o_ref, kall_ref, vall_ref, m_ref, acc_ref):
    tq = q_ref.shape[0]
    n = k_ref.shape[0]
    past = kc_ref.shape[3]
    total = n + past

    @pl.when(pl.program_id(2) == 0)
    def _():
        kall_ref[0:n, :] = k_ref[...]
        kall_ref[n:total, :] = kc_ref[0, 0, 0].astype(BF16)
        vall_ref[0:n, 0:HEAD_DIM] = v_ref[...]
        vall_ref[n:total, 0:HEAD_DIM] = vc_ref[0, 0, 0].astype(BF16)
        vall_ref[:, HEAD_DIM:2 * HEAD_DIM] = jnp.ones((total, HEAD_DIM), BF16)

    q = jnp.concatenate([q_ref[:, r * HEAD_DIM:(r + 1) * HEAD_DIM] for r in range(GQA_GROUP)], axis=0)
    m_ref[...] = jnp.full(m_ref.shape, -jnp.inf, F32)
    acc_ref[...] = jnp.zeros(acc_ref.shape, F32)

    def update(off, size):
        s = _nt(q, kall_ref[pl.ds(off, size), :])
        m_old = m_ref[...]
        m_new = jnp.maximum(m_old, s.max(axis=-1, keepdims=True))
        alpha = jnp.exp2(m_old - m_new)
        e = jnp.exp2(s - jnp.concatenate([m_new] * (size // LANES), axis=1))
        acc_ref[...] = (jnp.concatenate([alpha, alpha], axis=1) * acc_ref[...]
                        + _mm(e.astype(BF16), vall_ref[pl.ds(off, size), :]))
        m_ref[...] = m_new

    tk = min(GQA_KEY_CHUNK, total)
    full = total // tk

    def body(c, carry):
        update(pl.multiple_of(c * tk, tk), tk)
        return carry

    lax.fori_loop(0, full, body, 0, unroll=True)
    if total % tk:
        update(full * tk, total % tk)
    acc = acc_ref[...]
    o = acc[:, 0:HEAD_DIM] / acc[:, HEAD_DIM:2 * HEAD_DIM]
    for r in range(GQA_GROUP):
        o_ref[:, r * HEAD_DIM:(r + 1) * HEAD_DIM] = o[r * tq:(r + 1) * tq].astype(BF16)


def _gqa_latent(q, k, v, kc, vc, layer, batch, n):
    tq = min(256, n)
    nq = n // tq
    past = kc.shape[3]
    gw = GQA_GROUP * HEAD_DIM
    return pl.pallas_call(
        _gqa_lat_kernel,
        out_shape=jax.ShapeDtypeStruct((batch * n, GQ_W), BF16),
        grid=(batch, GQA_KV_HEADS, nq),
        in_specs=[
            pl.BlockSpec((tq, gw), lambda b, g, i: (b * nq + i, g)),
            pl.BlockSpec((n, HEAD_DIM), lambda b, g, i: (b, g)),
            pl.BlockSpec((n, HEAD_DIM), lambda b, g, i: (b, g)),
            pl.BlockSpec((1, 1, 1, past, HEAD_DIM), lambda b, g, i: (b, layer, g, 0, 0)),
            pl.BlockSpec((1, 1, 1, past, HEAD_DIM), lambda b, g, i: (b, layer, g, 0, 0)),
        ],
        out_specs=pl.BlockSpec((tq, gw), lambda b, g, i: (b * nq + i, g)),
        scratch_shapes=[pltpu.VMEM((n + past, HEAD_DIM), BF16), pltpu.VMEM((n + past, 2 * HEAD_DIM), BF16),
                        pltpu.VMEM((GQA_GROUP * tq, LANES), F32),
                        pltpu.VMEM((GQA_GROUP * tq, 2 * HEAD_DIM), F32)],
        compiler_params=_cp(3),
        name="gqa_latent",
    )(q, k, v, kc, vc)


def _outproj_kernel(x_ref, hy_ref, ona_ref, og_ref, mod_ref, g2_ref, wo_ref, rwh_ref, rwl_ref,
                    x1_ref, h2_ref, aff_ref):
    d = x_ref.shape[-1]
    mod = mod_ref[0]
    gate1 = mod[:, 2 * d:3 * d]
    shift2, scale2 = mod[:, 3 * d:4 * d], mod[:, 4 * d:5 * d]
    c1 = HY_WIDTH
    c2 = HY_WIDTH + NA_W
    c3 = c2 + GQ_W
    mixed = (_mm(hy_ref[...], wo_ref[0:c1, :]) + _mm(ona_ref[...], wo_ref[c1:c2, :])
             + _mm(og_ref[...], wo_ref[c2:c3, :]))
    x1 = x_ref[...] + gate1 * mixed
    x1_ref[...] = x1
    ms = jnp.mean(x1 * x1, axis=-1, keepdims=True)
    h2 = (x1 * lax.rsqrt(ms + NORM_EPS) * g2_ref[...]) * (1.0 + scale2) + shift2
    h2h = h2.astype(BF16)
    h2_ref[...] = h2h
    h2l = (h2 - h2h.astype(F32)).astype(BF16)
    logits = _nt(rwh_ref[...], h2h) + _nt(rwh_ref[...], h2l) + _nt(rwl_ref[...], h2h)
    m = logits.max(axis=0, keepdims=True)
    e = jnp.exp(logits - m)
    aff_ref[...] = e / e.sum(axis=0, keepdims=True)


def _out_projection(x2d, hy, ona, og, mod3, mod_base, seq, g2, w_out_bf, rw_hi, rw_lo, latent):
    t, d = x2d.shape
    tm = min(512, seq) if latent else min(512, t)
    tiles_per_batch = max(seq // tm, 1)
    if latent:
        mod_idx = lambda i: (mod_base + i // tiles_per_batch, 0, 0)
    else:
        mod_idx = lambda i: (mod_base, 0, 0)
    row = lambda w: pl.BlockSpec((tm, w), lambda i: (i, 0))
    ne = rw_hi.shape[0]
    return pl.pallas_call(
        _outproj_kernel,
        out_shape=[jax.ShapeDtypeStruct((t, d), F32), jax.ShapeDtypeStruct((t, d), BF16),
                   jax.ShapeDtypeStruct((ne, t), F32)],
        grid=(t // tm,),
        in_specs=[row(d), row(HY_WIDTH), row(NA_W), row(GQ_W),
                  pl.BlockSpec((1, 1, mod3.shape[-1]), mod_idx),
                  pl.BlockSpec((1, d), lambda i: (0, 0)),
                  pl.BlockSpec(w_out_bf.shape, lambda i: (0, 0), pipeline_mode=pl.Buffered(1)),
                  pl.BlockSpec((ne, d), lambda i: (0, 0)),
                  pl.BlockSpec((ne, d), lambda i: (0, 0))],
        out_specs=[row(d), row(d), pl.BlockSpec((ne, tm), lambda i: (0, i))],
        compiler_params=_cp(1),
        name="out_proj_lat" if latent else "out_proj_ctx",
    )(x2d, hy, ona, og, mod3, g2.reshape(1, d), w_out_bf, rw_hi, rw_lo)


ROUTE_CHUNK = 256


def _prefix_count(mask_f, tri):
    ne, n = mask_f.shape
    ch = tri.shape[0]
    pieces = []
    carry = jnp.zeros((ne, 1), F32)
    for c in range(n // ch):
        mk = mask_f[:, c * ch:(c + 1) * ch]
        inc = _mm(mk.astype(BF16), tri)
        pieces.append(inc - mk + carry)
        carry = carry + inc[:, ch - 1:ch]
    return (pieces[0] if len(pieces) == 1 else jnp.concatenate(pieces, axis=1)), carry


def _route_kernel(cap, aff_ref, slot_ref, slott_ref, gatet_ref):
    aff = aff_ref[...]
    ne, n = aff.shape

    def bisect(i, thr_bits):
        cand = thr_bits | (jnp.int32(1) << (30 - i))
        cnt = jnp.sum(jnp.where(aff >= pltpu.bitcast(cand, F32), 1.0, 0.0), axis=1, keepdims=True)
        return jnp.where(cnt >= cap, cand, thr_bits)

    thr = pltpu.bitcast(lax.fori_loop(0, 31, bisect, jnp.zeros((ne, 1), jnp.int32)), F32)
    ch = min(ROUTE_CHUNK, n)
    tri = jnp.where(lax.broadcasted_iota(jnp.int32, (ch, ch), 0) <= lax.broadcasted_iota(jnp.int32, (ch, ch), 1),
                    1.0, 0.0).astype(BF16)
    gt = jnp.where(aff > thr, 1.0, 0.0)
    eq = jnp.where(aff == thr, 1.0, 0.0)
    n_gt = jnp.sum(gt, axis=1, keepdims=True)
    eq_rank, _ = _prefix_count(eq, tri)
    sel = gt + eq * (eq_rank < (cap - n_gt)).astype(F32)
    rank, _ = _prefix_count(sel, tri)
    slot = jnp.where(sel > 0.0, rank, -1.0).astype(jnp.int32)
    slot_ref[...] = slot
    pad = jnp.zeros((LANES - ne, ch), F32)
    for c in range(n // ch):
        blk = jnp.concatenate([slot[:, c * ch:(c + 1) * ch].astype(F32), pad], axis=0)
        slott_ref[c * ch:(c + 1) * ch, :] = blk.T[:, :ne].astype(jnp.int32)
        gblk = jnp.concatenate([aff[:, c * ch:(c + 1) * ch], pad], axis=0)
        gatet_ref[c * ch:(c + 1) * ch, :] = gblk.T[:, :ne]


def _route(aff_t, n_sets, n):
    ne, t = aff_t.shape
    cap = EC_CAPACITY_FACTOR * n // N_EXPERTS
    return pl.pallas_call(
        functools.partial(_route_kernel, cap),
        out_shape=[jax.ShapeDtypeStruct((ne, t), jnp.int32), jax.ShapeDtypeStruct((t, ne), jnp.int32),
                   jax.ShapeDtypeStruct((t, ne), F32)],
        grid=(n_sets,),
        in_specs=[pl.BlockSpec((ne, n), lambda b: (0, b))],
        out_specs=[pl.BlockSpec((ne, n), lambda b: (0, b)), pl.BlockSpec((n, ne), lambda b: (b, 0)),
                   pl.BlockSpec((n, ne), lambda b: (b, 0))],
        compiler_params=_cp(1),
        name="route",
    )(aff_t)


GATHER_ROWS = 512


def _gather_kernel(slot_ref, h_ref, o_ref):
    eb, _, cap, _ = o_ref.shape
    n = slot_ref.shape[-1]
    srow = lax.broadcasted_iota(jnp.int32, (cap, n), 0)
    onehot = [jnp.where(slot_ref[e] == srow, 1.0, 0.0).astype(BF16) for e in range(eb)]
    onehot = onehot[0] if eb == 1 else jnp.concatenate(onehot, axis=0)
    xg = _mm(onehot, h_ref[...]).astype(BF16)
    for e in range(eb):
        o_ref[e, 0] = xg[e * cap:(e + 1) * cap]


def _gather(slot, h2, n_sets, n):
    ne, t = slot.shape
    d = h2.shape[1]
    cap = EC_CAPACITY_FACTOR * n // N_EXPERTS
    eb = min(ne, max(1, GATHER_ROWS // cap))
    td = min(d, 512)
    slot3 = slot.reshape(ne, 1, t)
    return pl.pallas_call(
        _gather_kernel,
        out_shape=jax.ShapeDtypeStruct((ne, n_sets, cap, d), BF16),
        grid=(n_sets, d // td, ne // eb),
        in_specs=[pl.BlockSpec((eb, 1, n), lambda b, j, e: (e, 0, b)),
                  pl.BlockSpec((n, td), lambda b, j, e: (b, j))],
        out_specs=pl.BlockSpec((eb, 1, cap, td), lambda b, j, e: (e, b, 0, j)),
        compiler_params=_cp(3),
        name="moe_gather",
    )(slot3, h2)


def _ffn_kernel(x_ref, wg_ref, wu_ref, wd_ref, y_ref, acc_ref):
    f = pl.program_id(2)

    @pl.when(f == 0)
    def _():
        acc_ref[...] = jnp.zeros_like(acc_ref)

    x = x_ref[0]
    a = _mm(x, wg_ref[0, 0].astype(BF16))
    u = _mm(x, wu_ref[0, 0].astype(BF16))
    hmid = (a * jax.nn.sigmoid(a) * u).astype(BF16)
    acc_ref[...] += _mm(hmid, wd_ref[0, 0].astype(BF16))

    @pl.when(f == pl.num_programs(2) - 1)
    def _():
        y_ref[0] = acc_ref[...].astype(BF16)


FFN_ROWS = 1024
FFN_COLS = 512


def _expert_ffn(xg, w_gate, w_up, w_down, layer):
    ne, m, d = xg.shape
    ff = w_gate.shape[-1]
    tf = min(ff, FFN_COLS)
    tm = min(m, FFN_ROWS)
    return pl.pallas_call(
        _ffn_kernel,
        out_shape=jax.ShapeDtypeStruct((ne, m, d), BF16),
        grid=(ne, m // tm, ff // tf),
        in_specs=[pl.BlockSpec((1, tm, d), lambda e, i, f: (e, i, 0)),
                  pl.BlockSpec((1, 1, d, tf), lambda e, i, f: (layer, e, 0, f)),
                  pl.BlockSpec((1, 1, d, tf), lambda e, i, f: (layer, e, 0, f)),
                  pl.BlockSpec((1, 1, tf, d), lambda e, i, f: (layer, e, f, 0))],
        out_specs=pl.BlockSpec((1, tm, d), lambda e, i, f: (e, i, 0)),
        scratch_shapes=[pltpu.VMEM((tm, d), F32)],
        compiler_params=_cp(3),
        name="moe_ffn",
    )(xg, w_gate, w_up, w_down)


def _scatter_kernel(fused, x_ref, slott_ref, gatet_ref, y_ref, gate2_ref, *rest):
    td = x_ref.shape[-1]
    ne, _, cap, _ = y_ref.shape
    tn = x_ref.shape[0]
    st = slott_ref[...]
    gt = gatet_ref[...]
    if fused:
        expand_ref, target_ref, o_ref = rest
        expand = expand_ref[...]
        slot_k = _mm(st.astype(F32).astype(BF16), expand)
        hit = slot_k == target_ref[...]
        g_hi = gt.astype(BF16)
        g_lo = (gt - g_hi.astype(F32)).astype(BF16)
        y_all = jnp.concatenate([y_ref[e, 0] for e in range(ne)], axis=0)
        moe = (_mm(jnp.where(hit, _mm(g_hi, expand), 0.0).astype(BF16), y_all)
               + _mm(jnp.where(hit, _mm(g_lo, expand), 0.0).astype(BF16), y_all))
    else:
        (o_ref,) = rest
        lane = lax.broadcasted_iota(jnp.int32, (tn, cap), 1)
        moe = jnp.zeros((tn, td), F32)
        for e in range(ne):
            onehot = jnp.where(st[:, e:e + 1] == lane, 1.0, 0.0).astype(BF16)
            moe = moe + gt[:, e:e + 1] * _mm(onehot, y_ref[e, 0])
    o_ref[...] = x_ref[...] + gate2_ref[0] * moe


SCATTER_FUSED_K = 512


def _scatter_residual(x1, slot_t, gate_t, y4, mod3, mod_base, n_sets, n, latent):
    t, d = x1.shape
    ne, _, cap, _ = y4.shape
    tn = min(n, 256)
    td = min(d, 512)
    npt = n // tn
    g2_blk = 5 * (d // td)
    if latent:
        mod_idx = lambda b, j, i: (mod_base + b, 0, g2_blk + j)
    else:
        mod_idx = lambda b, j, i: (mod_base, 0, g2_blk + j)
    in_specs = [pl.BlockSpec((tn, td), lambda b, j, i: (b * npt + i, j)),
                pl.BlockSpec((tn, ne), lambda b, j, i: (b * npt + i, 0)),
                pl.BlockSpec((tn, ne), lambda b, j, i: (b * npt + i, 0)),
                pl.BlockSpec((ne, 1, cap, td), lambda b, j, i: (0, b, 0, j)),
                pl.BlockSpec((1, 1, td), mod_idx)]
    args = [x1, slot_t, gate_t, y4, mod3]
    k = ne * cap
    fused = k <= SCATTER_FUSED_K
    if fused:
        cols = np.arange(k)
        expand = (cols[None, :] // cap == np.arange(ne)[:, None]).astype(np.float32)
        in_specs += [pl.BlockSpec((ne, k), lambda b, j, i: (0, 0)), pl.BlockSpec((1, k), lambda b, j, i: (0, 0))]
        args += [jnp.asarray(expand, F32).astype(BF16), jnp.asarray((cols % cap)[None, :], F32)]
    return pl.pallas_call(
        functools.partial(_scatter_kernel, fused),
        out_shape=jax.ShapeDtypeStruct((t, d), F32),
        grid=(n_sets, d // td, npt),
        in_specs=in_specs,
        out_specs=pl.BlockSpec((tn, td), lambda b, j, i: (b * npt + i, j)),
        compiler_params=_cp(3),
        name="moe_scatter",
    )(*args)


def _mixer_ctx(x2d, batch, seq, mod3, lp):
    hy, qna, qg, nak, nav, gk, gv = _in_projection(
        x2d, mod3, 0, batch, seq, lp["norm1_g"], lp["w_in_bf"], lp["head_gains"], latent=False)
    y_hy = _hyena(hy.reshape(batch, seq, HY_IN), lp)
    ona, og = _ctx_attention(qna, qg, nak, nav, gk, gv)
    x1, h2, aff = _out_projection(x2d, y_hy, ona, og, mod3, 0, seq, lp["norm2_g"], lp["w_out_bf"],
                                  lp["rw_hi"], lp["rw_lo"], latent=False)
    return x1, h2, aff, (nak, nav, gk, gv)


def _mixer_lat(x2d, batch, seq, mod3, lp, caches, layer):
    hy, qna, kna, vna, qg, kg, vg = _in_projection(
        x2d, mod3, 1, batch, seq, lp["norm1_g"], lp["w_in_bf"], lp["head_gains"], latent=True)
    y_hy = _hyena(hy.reshape(batch, seq, HY_IN), lp)
    na_kc, na_vc, g_kc, g_vc = caches
    ona = _na_latent(qna, kna, vna, na_kc, na_vc, layer, lp["na_rpb"], batch, seq)
    og = _gqa_latent(qg, kg, vg, g_kc, g_vc, layer, batch, seq)
    x1, h2, aff = _out_projection(x2d, y_hy, ona, og, mod3, 1, seq, lp["norm2_g"], lp["w_out_bf"],
                                  lp["rw_hi"], lp["rw_lo"], latent=True)
    return x1, h2, aff


def _moe(parts, mod3, lp):
    outs = []
    for x1, h2, aff, n_sets, n, mod_base, latent in parts:
        slot, slot_t, gate_t = _route(aff, n_sets, n)
        xg = _gather(slot, h2, n_sets, n)
        ne, _, cap, d = xg.shape
        y = _expert_ffn(xg.reshape(ne, n_sets * cap, d),
                        lp["exp_w_gate"], lp["exp_w_up"], lp["exp_w_down"], lp["layer"])
        outs.append(_scatter_residual(x1, slot_t, gate_t, y.reshape(xg.shape), mod3, mod_base, n_sets, n,
                                      latent))
    return outs


def kernel(x_prompt, x_sample, cache_na_k, cache_na_v, cache_gqa_k, cache_gqa_v, c, c_ctx, ada_w, ada_b, norm1_g, norm2_g, w_in, w_out, hy_short_w, hy_short_b, hy_f_w1, hy_f_b1, hy_f_w2, hy_f_b2, hy_f_w3, hy_f_freq, hy_bias, na_q_g, na_k_g, na_rpb, gqa_q_g, gqa_k_g, router_w, exp_w_gate, exp_w_up, exp_w_down):
    batch, seq, d = x_prompt.shape
    dbatch, dseq, _ = x_sample.shape
    depth = ada_w.shape[0]
    rows = 8 * ((1 + dbatch + 7) // 8)
    cvec = jnp.zeros((rows, d), F32).at[0].set(c_ctx).at[1:1 + dbatch].set(c)
    mod_all = _modulation(cvec, ada_w, ada_b)
    yp = x_prompt.reshape(batch * seq, d)
    ys = x_sample.reshape(dbatch * dseq, d)
    rw_t = jnp.swapaxes(router_w, 1, 2)
    rw_hi = rw_t.astype(BF16)
    rw_lo = (rw_t - rw_hi.astype(F32)).astype(BF16)
    new_kv = [[], [], [], []]
    for l in range(depth):
        lp = {
            "norm1_g": norm1_g[l], "norm2_g": norm2_g[l],
            "w_in_bf": w_in[l].astype(BF16), "w_out_bf": w_out[l].astype(BF16),
            "hy_short_w": hy_short_w[l], "hy_short_b": hy_short_b[l],
            "hy_f_w1": hy_f_w1[l], "hy_f_b1": hy_f_b1[l], "hy_f_w2": hy_f_w2[l], "hy_f_b2": hy_f_b2[l],
            "hy_f_w3": hy_f_w3[l], "hy_f_freq": hy_f_freq[l], "hy_bias": hy_bias[l],
            "head_gains": jnp.stack([na_q_g[l], na_k_g[l], gqa_q_g[l], gqa_k_g[l]]),
            "na_rpb": na_rpb[l], "rw_hi": rw_hi[l], "rw_lo": rw_lo[l],
            "layer": l, "exp_w_gate": exp_w_gate, "exp_w_up": exp_w_up, "exp_w_down": exp_w_down,
        }
        mod3 = mod_all[l].reshape(rows, 1, 6 * d)
        xp1, hp2, affp, kv = _mixer_ctx(yp, batch, seq, mod3, lp)
        for dst, src in zip(new_kv, kv):
            dst.append(src)
        xs1, hs2, affs = _mixer_lat(ys, dbatch, dseq, mod3, lp,
                                    (cache_na_k, cache_na_v, cache_gqa_k, cache_gqa_v), l)
        yp, ys = _moe([(xp1, hp2, affp, batch, seq, 0, False), (xs1, hs2, affs, dbatch, dseq, 1, True)],
                      mod3, lp)
    outs = [jnp.stack(v, axis=1) for v in new_kv]
    return (yp.reshape(batch, seq, d), ys.reshape(dbatch, dseq, d), outs[0], outs[1], outs[2], outs[3])
```

```python
import functools
import math

import numpy as np
import jax
import jax.numpy as jnp
from jax import lax
from jax.experimental import pallas as pl
from jax.experimental.pallas import tpu as pltpu

F32 = jnp.float32
BF16 = jnp.bfloat16

HEAD_DIM = 128
GRID_W = 64
HY_WIDTH = 512
HY_ORDER = 2
HY_IN = (HY_ORDER + 1) * HY_WIDTH
HY_POS_EMB = 33
HY_DECAY_TARGET = 1e-2
HY_FAST_PCT = 0.3
HY_SLOW_PCT = 1.5
NA_HEADS = 6
NA_WIN_H = 8
NA_WIN_W = 16
GQA_Q_HEADS = 6
GQA_KV_HEADS = 2
GQA_GROUP = GQA_Q_HEADS // GQA_KV_HEADS
ROPE_THETA = 10000.0
N_EXPERTS = 16
EC_CAPACITY_FACTOR = 2
NORM_EPS = 1e-6
MASK_VALUE = -1e30
NA_W = NA_HEADS * HEAD_DIM
GQ_W = GQA_Q_HEADS * HEAD_DIM
GKV_W = GQA_KV_HEADS * HEAD_DIM

LANES = 128
VMEM_LIMIT_BYTES = 56 * 1024 * 1024
LOG2_E = math.log2(math.e)
DFT_ROWS = 16
DFT_INNER = 128
DIRECT_DFT_MAX_LEN = 512


def _cp(n_axes, vmem=VMEM_LIMIT_BYTES):
    return pltpu.CompilerParams(dimension_semantics=("arbitrary",) * n_axes, vmem_limit_bytes=vmem)


def _nt(a, b):
    return lax.dot_general(a, b, (((1,), (1,)), ((), ())), preferred_element_type=F32)


def _mm(a, b):
    return jnp.dot(a, b, preferred_element_type=F32)


def _mm_hi(a, b):
    return jnp.dot(a, b, preferred_element_type=F32, precision=lax.Precision.HIGHEST)


def _mod_kernel(c_ref, w_ref, b_ref, o_ref):
    c = c_ref[...]
    s = c * jax.nn.sigmoid(c)
    o_ref[0] = _mm(s.astype(BF16), w_ref[0].astype(BF16)) + b_ref[0]


def _modulation(cvec, ada_w, ada_b):
    depth, d, n = ada_w.shape
    rows = cvec.shape[0]
    tn = min(n, 512)
    return pl.pallas_call(
        _mod_kernel,
        out_shape=jax.ShapeDtypeStruct((depth, rows, n), F32),
        grid=(depth, n // tn),
        in_specs=[
            pl.BlockSpec((rows, d), lambda l, j: (0, 0)),
            pl.BlockSpec((1, d, tn), lambda l, j: (l, 0, j)),
            pl.BlockSpec((1, 1, tn), lambda l, j: (l, 0, j)),
        ],
        out_specs=pl.BlockSpec((1, rows, tn), lambda l, j: (l, 0, j)),
        compiler_params=_cp(2),
        name="modulation",
    )(cvec, ada_w, ada_b.reshape(depth, 1, n))


def _head_norm(t, g):
    return t * lax.rsqrt(jnp.mean(t * t, axis=-1, keepdims=True) + NORM_EPS) * g


def _rope(t, cos, sin):
    swapped = jnp.where((lax.broadcasted_iota(jnp.int32, t.shape, 1) % 64) < 32,
                        pltpu.roll(t, HEAD_DIM - 32, 1), pltpu.roll(t, 32, 1))
    return t * cos + swapped * sin


def _inproj_kernel(latent, x_ref, mod_ref, g1_ref, w_ref, hg_ref, *rest):
    if latent:
        cos_ref, sin_ref, hy_ref, qna_ref, kna_ref, vna_ref, qg_ref, kg_ref, vg_ref = rest
    else:
        hy_ref, qna_ref, qg_ref, nak_ref, nav_ref, gk_ref, gv_ref = rest
    d = x_ref.shape[-1]
    x = x_ref[...]
    mod = mod_ref[0]
    shift, scale = mod[:, 0:d], mod[:, d:2 * d]
    ms = jnp.mean(x * x, axis=-1, keepdims=True)
    h = (x * lax.rsqrt(ms + NORM_EPS) * g1_ref[...]) * (1.0 + scale) + shift
    hb = h.astype(BF16)

    def proj(c0, n):
        return _mm(hb, w_ref[:, c0:c0 + n])

    hy_ref[...] = proj(0, HY_IN)
    q_scale = HEAD_DIM ** -0.5 * LOG2_E
    g_naq, g_nak = hg_ref[0:1, :], hg_ref[1:2, :]
    g_gq, g_gk = hg_ref[2:3, :], hg_ref[3:4, :]
    c = HY_IN
    z_naq = proj(c, NA_W)
    z_nak = proj(c + NA_W, NA_W)
    z_nav = proj(c + 2 * NA_W, NA_W)
    c += 3 * NA_W
    z_gq = proj(c, GQ_W)
    z_gk = proj(c + GQ_W, GKV_W)
    z_gv = proj(c + GQ_W + GKV_W, GKV_W)
    if latent:
        cos, sin = cos_ref[...], sin_ref[...]
    seq = None if latent else nak_ref.shape[2]
    nb = None if latent else nak_ref.shape[0]
    for hd in range(NA_HEADS):
        sl = slice(hd * HEAD_DIM, (hd + 1) * HEAD_DIM)
        qna_ref[:, sl] = (_head_norm(z_naq[:, sl], g_naq) * q_scale).astype(BF16)
        k = _head_norm(z_nak[:, sl], g_nak)
        v = z_nav[:, sl]
        if latent:
            kna_ref[:, sl] = k.astype(BF16)
            vna_ref[:, sl] = v.astype(BF16)
        else:
            for b in range(nb):
                nak_ref[b, hd] = k[b * seq:(b + 1) * seq]
                nav_ref[b, hd] = v[b * seq:(b + 1) * seq]
    for hd in range(GQA_Q_HEADS):
        sl = slice(hd * HEAD_DIM, (hd + 1) * HEAD_DIM)
        q = _head_norm(z_gq[:, sl], g_gq)
        if latent:
            q = _rope(q, cos, sin)
        qg_ref[:, sl] = (q * q_scale).astype(BF16)
    for hd in range(GQA_KV_HEADS):
        sl = slice(hd * HEAD_DIM, (hd + 1) * HEAD_DIM)
        k = _head_norm(z_gk[:, sl], g_gk)
        v = z_gv[:, sl]
        if latent:
            kg_ref[:, sl] = _rope(k, cos, sin).astype(BF16)
            vg_ref[:, sl] = v.astype(BF16)
        else:
            for b in range(nb):
                gk_ref[b, hd] = k[b * seq:(b + 1) * seq]
                gv_ref[b, hd] = v[b * seq:(b + 1) * seq]


def _rope_tables(n):
    pos = np.arange(n)
    row = (pos // GRID_W).astype(np.float64)
    col = (pos % GRID_W).astype(np.float64)
    quarter = HEAD_DIM // 4
    inv = ROPE_THETA ** (-np.arange(quarter, dtype=np.float64) / quarter)
    ang_r = row[:, None] * inv[None, :]
    ang_c = col[:, None] * inv[None, :]
    cos = np.concatenate([np.cos(ang_r), np.cos(ang_r), np.cos(ang_c), np.cos(ang_c)], axis=-1)
    sin = np.concatenate([-np.sin(ang_r), np.sin(ang_r), -np.sin(ang_c), np.sin(ang_c)], axis=-1)
    return jnp.asarray(cos, F32), jnp.asarray(sin, F32)


def _in_projection(x2d, mod3, mod_base, batch, seq, g1, w_in_bf, head_gains, latent):
    t, d = x2d.shape
    tm = min(512, seq) if latent else min(512, t)
    if not latent:
        tm = max(tm, seq)
    nt_ = t // tm
    tiles_per_batch = seq // tm if latent else None
    wide = w_in_bf.shape[1]
    if latent:
        mod_idx = lambda i: (mod_base + i // tiles_per_batch, 0, 0)
    else:
        mod_idx = lambda i: (mod_base, 0, 0)
    in_specs = [
        pl.BlockSpec((tm, d), lambda i: (i, 0)),
        pl.BlockSpec((1, 1, mod3.shape[-1]), mod_idx),
        pl.BlockSpec((1, d), lambda i: (0, 0)),
        pl.BlockSpec((d, wide), lambda i: (0, 0), pipeline_mode=pl.Buffered(1)),
        pl.BlockSpec((4, HEAD_DIM), lambda i: (0, 0)),
    ]
    args = [x2d, mod3, g1.reshape(1, d), w_in_bf, head_gains]
    row_spec = lambda w: pl.BlockSpec((tm, w), lambda i: (i, 0))
    if latent:
        cos, sin = _rope_tables(seq)
        in_specs += [pl.BlockSpec((tm, HEAD_DIM), lambda i: (i % tiles_per_batch, 0))] * 2
        args += [cos, sin]
        out_shape = [
            jax.ShapeDtypeStruct((t, HY_IN), F32),
            jax.ShapeDtypeStruct((t, NA_W), BF16), jax.ShapeDtypeStruct((t, NA_W), BF16),
            jax.ShapeDtypeStruct((t, NA_W), BF16), jax.ShapeDtypeStruct((t, GQ_W), BF16),
            jax.ShapeDtypeStruct((t, GKV_W), BF16), jax.ShapeDtypeStruct((t, GKV_W), BF16),
        ]
        out_specs = [row_spec(HY_IN), row_spec(NA_W), row_spec(NA_W), row_spec(NA_W),
                     row_spec(GQ_W), row_spec(GKV_W), row_spec(GKV_W)]
    else:
        nb = tm // seq
        kv_spec = lambda hh: pl.BlockSpec((nb, hh, seq, HEAD_DIM), lambda i: (i, 0, 0, 0))
        out_shape = [
            jax.ShapeDtypeStruct((t, HY_IN), F32),
            jax.ShapeDtypeStruct((t, NA_W), BF16), jax.ShapeDtypeStruct((t, GQ_W), BF16),
            jax.ShapeDtypeStruct((batch, NA_HEADS, seq, HEAD_DIM), F32),
            jax.ShapeDtypeStruct((batch, NA_HEADS, seq, HEAD_DIM), F32),
            jax.ShapeDtypeStruct((batch, GQA_KV_HEADS, seq, HEAD_DIM), F32),
            jax.ShapeDtypeStruct((batch, GQA_KV_HEADS, seq, HEAD_DIM), F32),
        ]
        out_specs = [row_spec(HY_IN), row_spec(NA_W), row_spec(GQ_W),
                     kv_spec(NA_HEADS), kv_spec(NA_HEADS), kv_spec(GQA_KV_HEADS), kv_spec(GQA_KV_HEADS)]
    return pl.pallas_call(
        functools.partial(_inproj_kernel, latent),
        out_shape=out_shape,
        grid=(nt_,),
        in_specs=in_specs,
        out_specs=out_specs,
        compiler_params=_cp(1),
        name="in_proj_lat" if latent else "in_proj_ctx",
    )(*args)


def _shortconv_kernel(x_ref, w_ref, b_ref, o_ref):
    o_ref[0, 0] = _short_conv_rows(x_ref[0], w_ref, b_ref)


def _short_conv(hy3, sw, sb):
    b, l, _ = hy3.shape
    tc = 256
    per = HY_WIDTH // tc
    return pl.pallas_call(
        _shortconv_kernel,
        out_shape=jax.ShapeDtypeStruct((HY_ORDER + 1, b, l, HY_WIDTH), F32),
        grid=(b, HY_IN // tc),
        in_specs=[
            pl.BlockSpec((1, l, tc), lambda i, q: (i, 0, q)),
            pl.BlockSpec((3, tc), lambda i, q: (0, q)),
            pl.BlockSpec((1, tc), lambda i, q: (0, q)),
        ],
        out_specs=pl.BlockSpec((1, 1, l, tc), lambda i, q: (q // per, i, 0, q % per)),
        compiler_params=_cp(2),
        name="hy_short_conv",
    )(hy3, sw, sb.reshape(1, HY_IN))


def _filter_kernel(z_ref, w1_ref, b1_ref, w2_ref, b2_ref, w3_ref, fr_ref, dl_ref, f_ref, n_ref):
    i = pl.program_id(0)
    z = z_ref[...]
    sf = fr_ref[...]
    hid = jnp.sin(sf * (_mm_hi(z, w1_ref[...]) + b1_ref[...]))
    hid = jnp.sin(sf * (_mm_hi(hid, w2_ref[...]) + b2_ref[...]))
    filt = _mm_hi(hid, w3_ref[...])
    t = z[:, 0:1]
    decay = jnp.exp(-t * dl_ref[...])
    decay = jnp.concatenate([decay] * (2 * HY_ORDER), axis=-1)
    filt = filt * decay
    half = HY_ORDER * HY_WIDTH
    rows = lax.broadcasted_iota(jnp.int32, filt.shape, 0) + i * filt.shape[0]
    cols = lax.broadcasted_iota(jnp.int32, filt.shape, 1)
    filt = jnp.where((rows == 0) & (cols >= half), 0.0, filt)
    f_ref[...] = filt
    part = jnp.sum(jnp.abs(filt), axis=0, keepdims=True)

    @pl.when(i == 0)
    def _():
        n_ref[...] = jnp.zeros_like(n_ref)

    n_ref[...] += part[:, :half] + part[:, half:]


def _filter_embedding(l):
    t = np.linspace(0.0, 1.0, l, dtype=np.float64)[:, None]
    bands = (HY_POS_EMB - 1) // 2
    fr = np.linspace(1e-4, bands - 1, bands, dtype=np.float64)[None, :]
    w = 2.0 * math.pi * np.arange(l, dtype=np.float64)[:, None] / l
    z = np.concatenate([t, np.cos(fr * w), -np.sin(fr * w)], axis=-1)
    zp = np.zeros((l, LANES), np.float32)
    zp[:, :HY_POS_EMB] = z
    min_d = abs(math.log(HY_DECAY_TARGET) / HY_SLOW_PCT)
    max_d = abs(math.log(HY_DECAY_TARGET) / HY_FAST_PCT)
    deltas = np.linspace(min_d, max_d, HY_WIDTH, dtype=np.float64)[None, :]
    return jnp.asarray(zp, F32), jnp.asarray(deltas, F32)


def _hyena_filters(l, w1, b1, w2, b2, w3, freq):
    zp, deltas = _filter_embedding(l)
    hid = w1.shape[1]
    w1p = jnp.zeros((LANES, hid), F32).at[:HY_POS_EMB].set(w1)
    tl = min(l, 512)
    wide = w3.shape[1]
    full = lambda a: pl.BlockSpec(a.shape, lambda i: (0,) * a.ndim)
    b1r, b2r, frr = b1.reshape(1, hid), b2.reshape(1, hid), freq.reshape(1, hid)
    return pl.pallas_call(
        _filter_kernel,
        out_shape=[jax.ShapeDtypeStruct((l, wide), F32),
                   jax.ShapeDtypeStruct((1, HY_ORDER * HY_WIDTH), F32)],
        grid=(l // tl,),
        in_specs=[pl.BlockSpec((tl, LANES), lambda i: (i, 0)), full(w1p), full(b1r), full(w2), full(b2r),
                  full(w3), full(frr), full(deltas)],
        out_specs=[pl.BlockSpec((tl, wide), lambda i: (i, 0)),
                   pl.BlockSpec((1, HY_ORDER * HY_WIDTH), lambda i: (0, 0))],
        compiler_params=_cp(1),
        name="hy_filters",
    )(zp, w1p, b1r, w2, b2r, w3, frr, deltas)


def _bf16_const(a):
    return jnp.asarray(a, F32).astype(BF16)


def _dft_plan(l):
    n = 2 * l
    n2_len = 1 if l <= DIRECT_DFT_MAX_LEN else DFT_INNER
    n1_len = n // n2_len
    k1 = np.arange(n1_len, dtype=np.float64)[:, None]
    n1 = np.arange(n1_len // 2, dtype=np.float64)[None, :]
    ang = 2.0 * np.pi * k1 * n1 / n1_len
    plan = {
        "n1": n1_len, "n2": n2_len,
        "fs": _bf16_const(np.concatenate([np.cos(ang), -np.sin(ang)], axis=0)),
        "gr": _bf16_const(np.cos(ang).T / n),
        "gi": _bf16_const(-np.sin(ang).T / n),
    }
    if n2_len > 1:
        k2 = np.arange(n2_len, dtype=np.float64)
        a2 = 2.0 * np.pi * np.outer(k2, k2) / n2_len
        fr, fi = np.cos(a2), -np.sin(a2)
        plan["d2"] = _bf16_const(np.block([[fr, -fi], [fi, fr]]))
        plan["d2c"] = _bf16_const(np.block([[fr, fi], [-fi, fr]]))
        eye = np.eye(DFT_ROWS)
        plan["fsk"] = _bf16_const(np.kron(np.concatenate([np.cos(ang), -np.sin(ang)], axis=0), eye))
        plan["grk"] = _bf16_const(np.kron(np.cos(ang).T / n, eye))
        plan["gik"] = _bf16_const(np.kron(-np.sin(ang).T / n, eye))
        at = 2.0 * np.pi * np.arange(n1_len, dtype=np.float64)[:, None] * k2[None, :] / n
        plan["twr"] = jnp.asarray(np.repeat(np.cos(at)[:, :, None], LANES, axis=2), F32)
        plan["twi"] = jnp.asarray(np.repeat(-np.sin(at)[:, :, None], LANES, axis=2), F32)
    return plan


def _dft1_kernel(z_ref, fs_ref, ar_ref, ai_ref):
    a = _mm(fs_ref[...], z_ref[0, 0].astype(BF16))
    n1 = ar_ref.shape[1]
    ar_ref[0] = a[:n1].astype(BF16)
    ai_ref[0] = a[n1:].astype(BF16)


def _dft1_rows_kernel(z_ref, fsk_ref, ar_ref, ai_ref):
    _, _, kk, mb, c = z_ref.shape
    n1 = ar_ref.shape[1]
    a = _mm(fsk_ref[...], z_ref[0, 0].reshape(kk * mb, c).astype(BF16))
    ar_ref[0] = a[:n1 * mb].reshape(n1, mb, c).astype(BF16)
    ai_ref[0] = a[n1 * mb:].reshape(n1, mb, c).astype(BF16)


def _dft_stage1(z, part, plan):
    n1 = plan["n1"]
    if plan["n2"] > 1:
        _, b, kk, n2, c = z.shape
        mb = DFT_ROWS
        tc = min(c, 512)
        return pl.pallas_call(
            _dft1_rows_kernel,
            out_shape=[jax.ShapeDtypeStruct((b, n1, n2, c), BF16)] * 2,
            grid=(b, n2 // mb, c // tc),
            in_specs=[pl.BlockSpec((1, 1, kk, mb, tc), lambda i, j, q: (part, i, 0, j, q)),
                      pl.BlockSpec(plan["fsk"].shape, lambda i, j, q: (0, 0))],
            out_specs=[pl.BlockSpec((1, n1, mb, tc), lambda i, j, q: (i, 0, j, q))] * 2,
            compiler_params=_cp(3),
            name="hy_dft_outer",
        )(z, plan["fsk"])
    z4 = z
    _, b, kk, nc = z4.shape
    tn = min(nc, 4096)
    return pl.pallas_call(
        _dft1_kernel,
        out_shape=[jax.ShapeDtypeStruct((b, n1, nc), BF16)] * 2,
        grid=(b, nc // tn),
        in_specs=[pl.BlockSpec((1, 1, kk, tn), lambda i, j: (part, i, 0, j)),
                  pl.BlockSpec((2 * n1, kk), lambda i, j: (0, 0))],
        out_specs=[pl.BlockSpec((1, n1, tn), lambda i, j: (i, 0, j))] * 2,
        compiler_params=_cp(2),
        name="hy_dft_outer",
    )(z4, plan["fs"])


def _twiddle_inner_dft(ar, ai, twr, twi, d2):
    c = ar.shape[-1]
    twr = jnp.concatenate([twr] * (c // LANES), axis=-1)
    twi = jnp.concatenate([twi] * (c // LANES), axis=-1)
    zr = ar * twr - ai * twi
    zi = ar * twi + ai * twr
    x = _mm(d2, jnp.concatenate([zr, zi], axis=0).astype(BF16))
    n2 = ar.shape[0]
    return x[:n2], x[n2:], twr, twi


def _spec_filter_kernel(two_stage, kb, ar_ref, ai_ref, nrm_ref, *rest):
    if two_stage:
        twr_ref, twi_ref, d2_ref, kr_ref, ki_ref = rest
    else:
        kr_ref, ki_ref = rest
    half = HY_ORDER * HY_WIDTH
    inv = 1.0 / nrm_ref[...]

    def combine(xr, xi):
        return (xr[:, :half] + xr[:, half:]) * inv, (xi[:, :half] - xi[:, half:]) * inv

    if two_stage:
        for j in range(kb):
            xr, xi, _, _ = _twiddle_inner_dft(ar_ref[0, j].astype(F32), ai_ref[0, j].astype(F32),
                                              twr_ref[j], twi_ref[j], d2_ref[...])
            kr, ki = combine(xr, xi)
            for o in range(HY_ORDER):
                kr_ref[o, j] = kr[:, o * HY_WIDTH:(o + 1) * HY_WIDTH]
                ki_ref[o, j] = ki[:, o * HY_WIDTH:(o + 1) * HY_WIDTH]
    else:
        kr, ki = combine(ar_ref[0].astype(F32), ai_ref[0].astype(F32))
        for o in range(HY_ORDER):
            kr_ref[o] = kr[:, o * HY_WIDTH:(o + 1) * HY_WIDTH]
            ki_ref[o] = ki[:, o * HY_WIDTH:(o + 1) * HY_WIDTH]


def _filter_spectrum(filt, norm, plan):
    l, wide = filt.shape
    n1, n2 = plan["n1"], plan["n2"]
    two_stage = n2 > 1
    zf = filt.reshape(1, 1, n1 // 2, n2, wide) if two_stage else filt.reshape(1, 1, n1 // 2, wide)
    ar, ai = _dft_stage1(zf, 0, plan)
    if two_stage:
        kb = 8
        a_spec = pl.BlockSpec((1, kb, n2, wide), lambda i: (0, i, 0, 0))
        in_specs = [a_spec, a_spec, pl.BlockSpec(norm.shape, lambda i: (0, 0)),
                    pl.BlockSpec((kb, n2, LANES), lambda i: (i, 0, 0)),
                    pl.BlockSpec((kb, n2, LANES), lambda i: (i, 0, 0)),
                    pl.BlockSpec((2 * n2, 2 * n2), lambda i: (0, 0))]
        args = [ar, ai, norm, plan["twr"], plan["twi"], plan["d2"]]
        out_shape = [jax.ShapeDtypeStruct((HY_ORDER, n1, n2, HY_WIDTH), F32)] * 2
        out_specs = [pl.BlockSpec((HY_ORDER, kb, n2, HY_WIDTH), lambda i: (0, i, 0, 0))] * 2
    else:
        kb = min(n1, 256)
        a_spec = pl.BlockSpec((1, kb, wide), lambda i: (0, i, 0))
        in_specs = [a_spec, a_spec, pl.BlockSpec(norm.shape, lambda i: (0, 0))]
        args = [ar, ai, norm]
        out_shape = [jax.ShapeDtypeStruct((HY_ORDER, n1, HY_WIDTH), F32)] * 2
        out_specs = [pl.BlockSpec((HY_ORDER, kb, HY_WIDTH), lambda i: (0, i, 0))] * 2
    return pl.pallas_call(
        functools.partial(_spec_filter_kernel, two_stage, kb),
        out_shape=out_shape,
        grid=(n1 // kb,),
        in_specs=in_specs,
        out_specs=out_specs,
        compiler_params=_cp(1),
        name="hy_filter_spectrum",
    )(*args)


def _spec_mul_kernel(kb, ar_ref, ai_ref, kr_ref, ki_ref, twr_ref, twi_ref, d2_ref, d2c_ref, br_ref, bi_ref):
    for j in range(kb):
        xr, xi, twr, twi = _twiddle_inner_dft(ar_ref[0, j].astype(F32), ai_ref[0, j].astype(F32),
                                              twr_ref[j], twi_ref[j], d2_ref[...])
        kr, ki = kr_ref[0, j], ki_ref[0, j]
        yr = xr * kr - xi * ki
        yi = xr * ki + xi * kr
        bm = _mm(d2c_ref[...], jnp.concatenate([yr, yi], axis=0).astype(BF16))
        n2 = xr.shape[0]
        br, bi = bm[:n2], bm[n2:]
        br_ref[0, j] = (br * twr + bi * twi).astype(BF16)
        bi_ref[0, j] = (bi * twr - br * twi).astype(BF16)


def _spectrum_multiply(ar, ai, kr, ki, order, plan):
    b, n1, n2, c = ar.shape
    kb = 8
    a_spec = pl.BlockSpec((1, kb, n2, c), lambda i, j: (i, j, 0, 0))
    k_spec = pl.BlockSpec((1, kb, n2, c), lambda i, j: (order, j, 0, 0))
    tw_spec = pl.BlockSpec((kb, n2, LANES), lambda i, j: (j, 0, 0))
    d_spec = pl.BlockSpec((2 * n2, 2 * n2), lambda i, j: (0, 0))
    return pl.pallas_call(
        functools.partial(_spec_mul_kernel, kb),
        out_shape=[jax.ShapeDtypeStruct((b, n1, n2, c), BF16)] * 2,
        grid=(b, n1 // kb),
        in_specs=[a_spec, a_spec, k_spec, k_spec, tw_spec, tw_spec, d_spec, d_spec],
        out_specs=[a_spec, a_spec],
        compiler_params=_cp(2),
        name="hy_spectrum_mul",
    )(ar, ai, kr, ki, plan["twr"], plan["twi"], plan["d2"], plan["d2c"])


def _idft1_rows_kernel(br_ref, bi_ref, grk_ref, gik_ref, z_ref, x_ref, bias_ref, o_ref):
    _, n1, mb, c = br_ref.shape
    conv = (_mm(grk_ref[...], br_ref[0].reshape(n1 * mb, c))
            + _mm(gik_ref[...], bi_ref[0].reshape(n1 * mb, c)))
    conv = conv.reshape(n1 // 2, mb, c)
    z = z_ref[0, 0]
    o_ref[0, 0] = (x_ref[0, 0] * (conv + z * bias_ref[...])).astype(o_ref.dtype)


def _idft_gate(br, bi, plan, z5, z_part, u5, x_part, bias_row, out_dtype):
    b, n1, n2, c = br.shape
    kk = n1 // 2
    mb = DFT_ROWS
    b_spec = pl.BlockSpec((1, n1, mb, c), lambda i, j: (i, 0, j, 0))
    g_spec = pl.BlockSpec(plan["grk"].shape, lambda i, j: (0, 0))
    return pl.pallas_call(
        _idft1_rows_kernel,
        out_shape=jax.ShapeDtypeStruct((1, b, kk, n2, c), out_dtype),
        grid=(b, n2 // mb),
        in_specs=[b_spec, b_spec, g_spec, g_spec,
                  pl.BlockSpec((1, 1, kk, mb, c), lambda i, j: (z_part, i, 0, j, 0)),
                  pl.BlockSpec((1, 1, kk, mb, c), lambda i, j: (x_part, i, 0, j, 0)),
                  pl.BlockSpec((1, c), lambda i, j: (0, 0))],
        out_specs=pl.BlockSpec((1, 1, kk, mb, c), lambda i, j: (0, i, 0, j, 0)),
        compiler_params=_cp(2),
        name="hy_idft_gate",
    )(br, bi, plan["grk"], plan["gik"], z5, u5, bias_row)


def _short_conv_rows(x, w_ref, b_ref):
    n = x.shape[0]
    rows = lax.broadcasted_iota(jnp.int32, x.shape, 0)
    prev = jnp.where(rows == 0, 0.0, pltpu.roll(x, 1, 0))
    nxt = jnp.where(rows == n - 1, 0.0, pltpu.roll(x, n - 1, 0))
    return prev * w_ref[0:1, :] + x * w_ref[1:2, :] + nxt * w_ref[2:3, :] + b_ref[...]


def _hyena_direct_kernel(hy_ref, sw_ref, sb_ref, fs_ref, gr_ref, gi_ref, kr_ref, ki_ref, hb_ref, o_ref):
    u = _short_conv_rows(hy_ref[0], sw_ref, sb_ref)
    nfreq = kr_ref.shape[1]
    z = u[:, 0:HY_WIDTH]
    for o in range(HY_ORDER):
        a = _mm(fs_ref[...], z.astype(BF16))
        ar, ai = a[:nfreq], a[nfreq:]
        kr, ki = kr_ref[o], ki_ref[o]
        br = (ar * kr - ai * ki).astype(BF16)
        bi = (ar * ki + ai * kr).astype(BF16)
        conv = _mm(gr_ref[...], br) + _mm(gi_ref[...], bi)
        z = u[:, (o + 1) * HY_WIDTH:(o + 2) * HY_WIDTH] * (conv + z * hb_ref[o:o + 1, :])
    o_ref[...] = z.astype(o_ref.dtype)


def _hyena_direct(hy3, lp, plan, kr, ki):
    b, l, wide = hy3.shape
    n1 = plan["n1"]
    full = lambda a: pl.BlockSpec(a.shape, lambda i: (0,) * a.ndim)
    sb = lp["hy_short_b"].reshape(1, wide)
    args = [lp["hy_short_w"], sb, plan["fs"], plan["gr"], plan["gi"], kr, ki, lp["hy_bias"]]
    return pl.pallas_call(
        _hyena_direct_kernel,
        out_shape=jax.ShapeDtypeStruct((b * l, HY_WIDTH), BF16),
        grid=(b,),
        in_specs=[pl.BlockSpec((1, l, wide), lambda i: (i, 0, 0))] + [full(a) for a in args],
        out_specs=pl.BlockSpec((l, HY_WIDTH), lambda i: (i, 0)),
        compiler_params=_cp(1),
        name="hy_direct",
    )(hy3, *args)


def _hyena(hy3, lp):
    b, l, _ = hy3.shape
    plan = _dft_plan(l)
    n1, n2 = plan["n1"], plan["n2"]
    filt, norm = _hyena_filters(l, lp["hy_f_w1"], lp["hy_f_b1"], lp["hy_f_w2"], lp["hy_f_b2"],
                                lp["hy_f_w3"], lp["hy_f_freq"])
    kr, ki = _filter_spectrum(filt, norm, plan)
    if n2 == 1:
        return _hyena_direct(hy3, lp, plan, kr, ki)
    u = _short_conv(hy3, lp["hy_short_w"], lp["hy_short_b"])
    u5 = u.reshape(HY_ORDER + 1, b, n1 // 2, n2, HY_WIDTH)
    z5, z_part = u5, 0
    for o in range(HY_ORDER):
        ar, ai = _dft_stage1(z5, z_part, plan)
        br, bi = _spectrum_multiply(ar, ai, kr, ki, o, plan)
        bias_row = lp["hy_bias"][o].reshape(1, HY_WIDTH)
        last = o == HY_ORDER - 1
        z5 = _idft_gate(br, bi, plan, z5, z_part, u5, o + 1, bias_row, BF16 if last else F32)
        z_part = 0
    return z5.reshape(b * l, HY_WIDTH)


def _softmax_pv(s_list, v_list):
    m = s_list[0].max(axis=-1, keepdims=True)
    for s in s_list[1:]:
        m = jnp.maximum(m, s.max(axis=-1, keepdims=True))
    den = 0.0
    o = 0.0
    for s, v in zip(s_list, v_list):
        e = jnp.exp2(s - m)
        den = den + e.sum(axis=-1, keepdims=True)
        o = o + _mm(e.astype(BF16), v)
    return o / den


def _ctx_attn_kernel(qna_ref, qg_ref, nak_ref, nav_ref, gk_ref, gv_ref, ona_ref, og_ref):
    for hd in range(NA_HEADS):
        sl = slice(hd * HEAD_DIM, (hd + 1) * HEAD_DIM)
        k = nak_ref[0, hd].astype(BF16)
        v = nav_ref[0, hd].astype(BF16)
        ona_ref[:, sl] = _softmax_pv([_nt(qna_ref[:, sl], k)], [v]).astype(BF16)
    for g in range(GQA_KV_HEADS):
        k = gk_ref[0, g].astype(BF16)
        v = gv_ref[0, g].astype(BF16)
        for r in range(GQA_GROUP):
            hd = g * GQA_GROUP + r
            sl = slice(hd * HEAD_DIM, (hd + 1) * HEAD_DIM)
            og_ref[:, sl] = _softmax_pv([_nt(qg_ref[:, sl], k)], [v]).astype(BF16)


def _ctx_attention(qna, qg, nak, nav, gk, gv):
    batch, _, seq, _ = nak.shape
    t = batch * seq
    kv = lambda hh: pl.BlockSpec((1, hh, seq, HEAD_DIM), lambda i: (i, 0, 0, 0))
    row = lambda w: pl.BlockSpec((seq, w), lambda i: (i, 0))
    return pl.pallas_call(
        _ctx_attn_kernel,
        out_shape=[jax.ShapeDtypeStruct((t, NA_W), BF16), jax.ShapeDtypeStruct((t, GQ_W), BF16)],
        grid=(batch,),
        in_specs=[row(NA_W), row(GQ_W), kv(NA_HEADS), kv(NA_HEADS), kv(GQA_KV_HEADS), kv(GQA_KV_HEADS)],
        out_specs=[row(NA_W), row(GQ_W)],
        compiler_params=_cp(1),
        name="ctx_attention",
    )(qna, qg, nak, nav, gk, gv)


NA_Q_ROWS = 2


def _na_tables(rows):
    r_q = NA_Q_ROWS
    kh = min(NA_WIN_H, rows)
    win = min(r_q + kh, rows)
    nblk = rows // r_q
    starts, var_ids, variants, keys = [], [], [], {}
    qr = np.arange(r_q)[:, None]
    kr = np.arange(win)[None, :]
    for j in range(nblk):
        start = int(np.clip(r_q * j - kh // 2, 0, rows - win))
        r = r_q * j + qr
        rs = np.clip(r - kh // 2, 0, rows - kh)
        kabs = start + kr
        row_ok = (kabs >= rs) & (kabs < rs + kh)
        ridx = np.where(row_ok, kabs - r + NA_WIN_H - 1, 0)
        key = ridx.tobytes() + row_ok.tobytes()
        if key not in keys:
            keys[key] = len(variants)
            variants.append((ridx, row_ok))
        starts.append(start)
        var_ids.append(keys[key])
    ridx = np.stack([v[0] for v in variants])
    row_ok = np.stack([v[1] for v in variants])
    qc = np.arange(GRID_W)[:, None]
    kc = np.arange(GRID_W)[None, :]
    cstart = np.clip(qc - NA_WIN_W // 2, 0, GRID_W - NA_WIN_W)
    col_ok = (kc >= cstart) & (kc < cstart + NA_WIN_W)
    return win, np.asarray(starts, np.int32), np.asarray(var_ids, np.int32), ridx, row_ok, col_ok


def _na_bias(rpb, ridx, row_ok, col_ok):
    h, nd, nrel = rpb.shape
    half = NA_WIN_W - 1
    period = 2 * GRID_W
    w = jnp.concatenate([rpb[..., half:], jnp.zeros((h, nd, period - nrel), F32), rpb[..., :half]], axis=-1)
    toep = jnp.tile(w, (1, 1, GRID_W))[..., :GRID_W * (period - 1)]
    toep = toep.reshape(h, nd, GRID_W, period - 1)[..., :GRID_W]
    toep = jnp.where(jnp.asarray(col_ok), toep * LOG2_E, MASK_VALUE)
    nvar, r_q, win = ridx.shape
    blocks = jnp.take(toep, jnp.asarray(ridx.reshape(-1)), axis=1)
    blocks = blocks.reshape(h, nvar, r_q, win, GRID_W, GRID_W)
    blocks = jnp.where(jnp.asarray(row_ok)[None, :, :, :, None, None], blocks, MASK_VALUE)
    return blocks.transpose(0, 1, 2, 4, 3, 5).reshape(h, nvar, r_q * GRID_W, win * GRID_W)


def _na_lat_kernel(win, start_ref, var_ref, q_ref, k_ref, v_ref, kc_ref, vc_ref, bias_ref, o_ref):
    j = pl.program_id(1)
    off = pl.multiple_of(start_ref[j] * GRID_W, GRID_W)
    for hd in range(NA_HEADS):
        sl = slice(hd * HEAD_DIM, (hd + 1) * HEAD_DIM)
        q = q_ref[:, sl]
        kw = k_ref[pl.ds(off, win * GRID_W), sl]
        vw = v_ref[pl.ds(off, win * GRID_W), sl]
        s_loc = _nt(q, kw) + bias_ref[hd, 0]
        s_ctx = _nt(q, kc_ref[0, 0, hd].astype(BF16))
        o_ref[:, sl] = _softmax_pv([s_loc, s_ctx], [vw, vc_ref[0, 0, hd].astype(BF16)]).astype(BF16)


def _na_latent(q, k, v, kc, vc, layer, rpb, batch, n):
    rows = n // GRID_W
    win, starts, var_ids, ridx, row_ok, col_ok = _na_tables(rows)
    bias = _na_bias(rpb, ridx, row_ok, col_ok)
    r_q = NA_Q_ROWS
    nblk = rows // r_q
    tq = r_q * GRID_W
    wk = win * GRID_W
    past = kc.shape[3]
    grid_spec = pltpu.PrefetchScalarGridSpec(
        num_scalar_prefetch=2,
        grid=(batch, nblk),
        in_specs=[
            pl.BlockSpec((tq, NA_W), lambda b, j, st, vr: (b * nblk + j, 0)),
            pl.BlockSpec((n, NA_W), lambda b, j, st, vr: (b, 0)),
            pl.BlockSpec((n, NA_W), lambda b, j, st, vr: (b, 0)),
            pl.BlockSpec((1, 1, NA_HEADS, past, HEAD_DIM), lambda b, j, st, vr: (b, layer, 0, 0, 0)),
            pl.BlockSpec((1, 1, NA_HEADS, past, HEAD_DIM), lambda b, j, st, vr: (b, layer, 0, 0, 0)),
            pl.BlockSpec((NA_HEADS, 1, tq, wk), lambda b, j, st, vr: (0, vr[j], 0, 0)),
        ],
        out_specs=pl.BlockSpec((tq, NA_W), lambda b, j, st, vr: (b * nblk + j, 0)),
    )
    return pl.pallas_call(
        functools.partial(_na_lat_kernel, win),
        out_shape=jax.ShapeDtypeStruct((batch * n, NA_W), BF16),
        grid_spec=grid_spec,
        compiler_params=_cp(2),
        name="na_latent",
    )(jnp.asarray(starts), jnp.asarray(var_ids), q, k, v, kc, vc, bias)


GQA_KEY_CHUNK = 512


def _gqa_lat_kernel(q_ref, k_ref, v_ref, kc_ref, vc_ref, o_ref, kall_ref, vall_ref, m_ref, acc_ref):
    tq = q_ref.shape[0]
    n = k_ref.shape[0]
    past = kc_ref.shape[3]
    total = n + past

    @pl.when(pl.program_id(2) == 0)
    def _():
        kall_ref[0:n, :] = k_ref[...]
        kall_ref[n:total, :] = kc_ref[0, 0, 0].astype(BF16)
        vall_ref[0:n, 0:HEAD_DIM] = v_ref[...]
        vall_ref[n:total, 0:HEAD_DIM] = vc_ref[0, 0, 0].astype(BF16)
        vall_ref[:, HEAD_DIM:2 * HEAD_DIM] = jnp.ones((total, HEAD_DIM), BF16)

    q = jnp.concatenate([q_ref[:, r * HEAD_DIM:(r + 1) * HEAD_DIM] for r in range(GQA_GROUP)], axis=0)
    m_ref[...] = jnp.full(m_ref.shape, -jnp.inf, F32)
    acc_ref[...] = jnp.zeros(acc_ref.shape, F32)

    def update(off, size):
        s = _nt(q, kall_ref[pl.ds(off, size), :])
        m_old = m_ref[...]
        m_new = jnp.maximum(m_old, s.max(axis=-1, keepdims=True))
        alpha = jnp.exp2(m_old - m_new)
        e = jnp.exp2(s - jnp.concatenate([m_new] * (size // LANES), axis=1))
        acc_ref[...] = (jnp.concatenate([alpha, alpha], axis=1) * acc_ref[...]
                        + _mm(e.astype(BF16), vall_ref[pl.ds(off, size), :]))
        m_ref[...] = m_new

    tk = min(GQA_KEY_CHUNK, total)
    full = total // tk

    def body(c, carry):
        update(pl.multiple_of(c * tk, tk), tk)
        return carry

    lax.fori_loop(0, full, body, 0, unroll=True)
    if total % tk:
        update(full * tk, total % tk)
    acc = acc_ref[...]
    o = acc[:, 0:HEAD_DIM] / acc[:, HEAD_DIM:2 * HEAD_DIM]
    for r in range(GQA_GROUP):
        o_ref[:, r * HEAD_DIM:(r + 1) * HEAD_DIM] = o[r * tq:(r + 1) * tq].astype(BF16)


def _gqa_latent(q, k, v, kc, vc, layer, batch, n):
    tq = min(256, n)
    nq = n // tq
    past = kc.shape[3]
    gw = GQA_GROUP * HEAD_DIM
    return pl.pallas_call(
        _gqa_lat_kernel,
        out_shape=jax.ShapeDtypeStruct((batch * n, GQ_W), BF16),
        grid=(batch, GQA_KV_HEADS, nq),
        in_specs=[
            pl.BlockSpec((tq, gw), lambda b, g, i: (b * nq + i, g)),
            pl.BlockSpec((n, HEAD_DIM), lambda b, g, i: (b, g)),
            pl.BlockSpec((n, HEAD_DIM), lambda b, g, i: (b, g)),
            pl.BlockSpec((1, 1, 1, past, HEAD_DIM), lambda b, g, i: (b, layer, g, 0, 0)),
            pl.BlockSpec((1, 1, 1, past, HEAD_DIM), lambda b, g, i: (b, layer, g, 0, 0)),
        ],
        out_specs=pl.BlockSpec((tq, gw), lambda b, g, i: (b * nq + i, g)),
        scratch_shapes=[pltpu.VMEM((n + past, HEAD_DIM), BF16), pltpu.VMEM((n + past, 2 * HEAD_DIM), BF16),
                        pltpu.VMEM((GQA_GROUP * tq, LANES), F32),
                        pltpu.VMEM((GQA_GROUP * tq, 2 * HEAD_DIM), F32)],
        compiler_params=_cp(3),
        name="gqa_latent",
    )(q, k, v, kc, vc)


def _outproj_kernel(x_ref, hy_ref, ona_ref, og_ref, mod_ref, g2_ref, wo_ref, rwh_ref, rwl_ref,
                    x1_ref, h2_ref, aff_ref):
    d = x_ref.shape[-1]
    mod = mod_ref[0]
    gate1 = mod[:, 2 * d:3 * d]
    shift2, scale2 = mod[:, 3 * d:4 * d], mod[:, 4 * d:5 * d]
    c1 = HY_WIDTH
    c2 = HY_WIDTH + NA_W
    c3 = c2 + GQ_W
    mixed = (_mm(hy_ref[...], wo_ref[0:c1, :]) + _mm(ona_ref[...], wo_ref[c1:c2, :])
             + _mm(og_ref[...], wo_ref[c2:c3, :]))
    x1 = x_ref[...] + gate1 * mixed
    x1_ref[...] = x1
    ms = jnp.mean(x1 * x1, axis=-1, keepdims=True)
    h2 = (x1 * lax.rsqrt(ms + NORM_EPS) * g2_ref[...]) * (1.0 + scale2) + shift2
    h2h = h2.astype(BF16)
    h2_ref[...] = h2h
    h2l = (h2 - h2h.astype(F32)).astype(BF16)
    logits = _nt(rwh_ref[...], h2h) + _nt(rwh_ref[...], h2l) + _nt(rwl_ref[...], h2h)
    m = logits.max(axis=0, keepdims=True)
    e = jnp.exp(logits - m)
    aff_ref[...] = e / e.sum(axis=0, keepdims=True)


def _out_projection(x2d, hy, ona, og, mod3, mod_base, seq, g2, w_out_bf, rw_hi, rw_lo, latent):
    t, d = x2d.shape
    tm = min(512, seq) if latent else min(512, t)
    tiles_per_batch = max(seq // tm, 1)
    if latent:
        mod_idx = lambda i: (mod_base + i // tiles_per_batch, 0, 0)
    else:
        mod_idx = lambda i: (mod_base, 0, 0)
    row = lambda w: pl.BlockSpec((tm, w), lambda i: (i, 0))
    ne = rw_hi.shape[0]
    return pl.pallas_call(
        _outproj_kernel,
        out_shape=[jax.ShapeDtypeStruct((t, d), F32), jax.ShapeDtypeStruct((t, d), BF16),
                   jax.ShapeDtypeStruct((ne, t), F32)],
        grid=(t // tm,),
        in_specs=[row(d), row(HY_WIDTH), row(NA_W), row(GQ_W),
                  pl.BlockSpec((1, 1, mod3.shape[-1]), mod_idx),
                  pl.BlockSpec((1, d), lambda i: (0, 0)),
                  pl.BlockSpec(w_out_bf.shape, lambda i: (0, 0), pipeline_mode=pl.Buffered(1)),
                  pl.BlockSpec((ne, d), lambda i: (0, 0)),
                  pl.BlockSpec((ne, d), lambda i: (0, 0))],
        out_specs=[row(d), row(d), pl.BlockSpec((ne, tm), lambda i: (0, i))],
        compiler_params=_cp(1),
        name="out_proj_lat" if latent else "out_proj_ctx",
    )(x2d, hy, ona, og, mod3, g2.reshape(1, d), w_out_bf, rw_hi, rw_lo)


ROUTE_CHUNK = 256


def _prefix_count(mask_f, tri):
    ne, n = mask_f.shape
    ch = tri.shape[0]
    pieces = []
    carry = jnp.zeros((ne, 1), F32)
    for c in range(n // ch):
        mk = mask_f[:, c * ch:(c + 1) * ch]
        inc = _mm(mk.astype(BF16), tri)
        pieces.append(inc - mk + carry)
        carry = carry + inc[:, ch - 1:ch]
    return (pieces[0] if len(pieces) == 1 else jnp.concatenate(pieces, axis=1)), carry


def _route_kernel(cap, aff_ref, slot_ref, slott_ref, gatet_ref):
    aff = aff_ref[...]
    ne, n = aff.shape

    def bisect(i, thr_bits):
        cand = thr_bits | (jnp.int32(1) << (30 - i))
        cnt = jnp.sum(jnp.where(aff >= pltpu.bitcast(cand, F32), 1.0, 0.0), axis=1, keepdims=True)
        return jnp.where(cnt >= cap, cand, thr_bits)

    thr = pltpu.bitcast(lax.fori_loop(0, 31, bisect, jnp.zeros((ne, 1), jnp.int32)), F32)
    ch = min(ROUTE_CHUNK, n)
    tri = jnp.where(lax.broadcasted_iota(jnp.int32, (ch, ch), 0) <= lax.broadcasted_iota(jnp.int32, (ch, ch), 1),
                    1.0, 0.0).astype(BF16)
    gt = jnp.where(aff > thr, 1.0, 0.0)
    eq = jnp.where(aff == thr, 1.0, 0.0)
    n_gt = jnp.sum(gt, axis=1, keepdims=True)
    eq_rank, _ = _prefix_count(eq, tri)
    sel = gt + eq * (eq_rank < (cap - n_gt)).astype(F32)
    rank, _ = _prefix_count(sel, tri)
    slot = jnp.where(sel > 0.0, rank, -1.0).astype(jnp.int32)
    slot_ref[...] = slot
    pad = jnp.zeros((LANES - ne, ch), F32)
    for c in range(n // ch):
        blk = jnp.concatenate([slot[:, c * ch:(c + 1) * ch].astype(F32), pad], axis=0)
        slott_ref[c * ch:(c + 1) * ch, :] = blk.T.astype(jnp.int32)
        gblk = jnp.concatenate([aff[:, c * ch:(c + 1) * ch], pad], axis=0)
        gatet_ref[c * ch:(c + 1) * ch, :] = gblk.T


def _route(aff_t, n_sets, n):
    ne, t = aff_t.shape
    cap = EC_CAPACITY_FACTOR * n // N_EXPERTS
    return pl.pallas_call(
        functools.partial(_route_kernel, cap),
        out_shape=[jax.ShapeDtypeStruct((ne, t), jnp.int32), jax.ShapeDtypeStruct((t, LANES), jnp.int32),
                   jax.ShapeDtypeStruct((t, LANES), F32)],
        grid=(n_sets,),
        in_specs=[pl.BlockSpec((ne, n), lambda b: (0, b))],
        out_specs=[pl.BlockSpec((ne, n), lambda b: (0, b)), pl.BlockSpec((n, LANES), lambda b: (b, 0)),
                   pl.BlockSpec((n, LANES), lambda b: (b, 0))],
        compiler_params=_cp(1),
        name="route",
    )(aff_t)


GATHER_ROWS = 512


def _gather_kernel(slot_ref, h_ref, o_ref):
    eb, _, cap, _ = o_ref.shape
    n = slot_ref.shape[-1]
    srow = lax.broadcasted_iota(jnp.int32, (cap, n), 0)
    onehot = [jnp.where(slot_ref[e] == srow, 1.0, 0.0).astype(BF16) for e in range(eb)]
    onehot = onehot[0] if eb == 1 else jnp.concatenate(onehot, axis=0)
    xg = _mm(onehot, h_ref[...]).astype(BF16)
    for e in range(eb):
        o_ref[e, 0] = xg[e * cap:(e + 1) * cap]


def _gather(slot, h2, n_sets, n):
    ne, t = slot.shape
    d = h2.shape[1]
    cap = EC_CAPACITY_FACTOR * n // N_EXPERTS
    eb = min(ne, max(1, GATHER_ROWS // cap))
    td = min(d, 512)
    slot3 = slot.reshape(ne, 1, t)
    return pl.pallas_call(
        _gather_kernel,
        out_shape=jax.ShapeDtypeStruct((ne, n_sets, cap, d), BF16),
        grid=(n_sets, d // td, ne // eb),
        in_specs=[pl.BlockSpec((eb, 1, n), lambda b, j, e: (e, 0, b)),
                  pl.BlockSpec((n, td), lambda b, j, e: (b, j))],
        out_specs=pl.BlockSpec((eb, 1, cap, td), lambda b, j, e: (e, b, 0, j)),
        compiler_params=_cp(3),
        name="moe_gather",
    )(slot3, h2)


def _ffn_kernel(x_ref, wg_ref, wu_ref, wd_ref, y_ref, acc_ref):
    f = pl.program_id(2)

    @pl.when(f == 0)
    def _():
        acc_ref[...] = jnp.zeros_like(acc_ref)

    x = x_ref[0]
    a = _mm(x, wg_ref[0, 0].astype(BF16))
    u = _mm(x, wu_ref[0, 0].astype(BF16))
    hmid = (a * jax.nn.sigmoid(a) * u).astype(BF16)
    acc_ref[...] += _mm(hmid, wd_ref[0, 0].astype(BF16))

    @pl.when(f == pl.num_programs(2) - 1)
    def _():
        y_ref[0] = acc_ref[...].astype(BF16)


FFN_ROWS = 1024
FFN_COLS = 512


def _expert_ffn(xg, w_gate, w_up, w_down, layer):
    ne, m, d = xg.shape
    ff = w_gate.shape[-1]
    tf = min(ff, FFN_COLS)
    tm = min(m, FFN_ROWS)
    return pl.pallas_call(
        _ffn_kernel,
        out_shape=jax.ShapeDtypeStruct((ne, m, d), BF16),
        grid=(ne, m // tm, ff // tf),
        in_specs=[pl.BlockSpec((1, tm, d), lambda e, i, f: (e, i, 0)),
                  pl.BlockSpec((1, 1, d, tf), lambda e, i, f: (layer, e, 0, f)),
                  pl.BlockSpec((1, 1, d, tf), lambda e, i, f: (layer, e, 0, f)),
                  pl.BlockSpec((1, 1, tf, d), lambda e, i, f: (layer, e, f, 0))],
        out_specs=pl.BlockSpec((1, tm, d), lambda e, i, f: (e, i, 0)),
        scratch_shapes=[pltpu.VMEM((tm, d), F32)],
        compiler_params=_cp(3),
        name="moe_ffn",
    )(xg, w_gate, w_up, w_down)


def _scatter_kernel(fused, x_ref, slott_ref, gatet_ref, y_ref, gate2_ref, *rest):
    td = x_ref.shape[-1]
    ne, _, cap, _ = y_ref.shape
    tn = x_ref.shape[0]
    st = slott_ref[:, 0:ne]
    gt = gatet_ref[:, 0:ne]
    if fused:
        expand_ref, target_ref, o_ref = rest
        expand = expand_ref[...]
        slot_k = _mm(st.astype(F32).astype(BF16), expand)
        hit = slot_k == target_ref[...]
        g_hi = gt.astype(BF16)
        g_lo = (gt - g_hi.astype(F32)).astype(BF16)
        y_all = jnp.concatenate([y_ref[e, 0] for e in range(ne)], axis=0)
        moe = (_mm(jnp.where(hit, _mm(g_hi, expand), 0.0).astype(BF16), y_all)
               + _mm(jnp.where(hit, _mm(g_lo, expand), 0.0).astype(BF16), y_all))
    else:
        (o_ref,) = rest
        lane = lax.broadcasted_iota(jnp.int32, (tn, cap), 1)
        moe = jnp.zeros((tn, td), F32)
        for e in range(ne):
            onehot = jnp.where(st[:, e:e + 1] == lane, 1.0, 0.0).astype(BF16)
            moe = moe + gt[:, e:e + 1] * _mm(onehot, y_ref[e, 0])
    o_ref[...] = x_ref[...] + gate2_ref[0] * moe


SCATTER_FUSED_K = 512


def _scatter_residual(x1, slot_t, gate_t, y4, mod3, mod_base, n_sets, n, latent):
    t, d = x1.shape
    ne, _, cap, _ = y4.shape
    tn = min(n, 256)
    td = min(d, 512)
    npt = n // tn
    g2_blk = 5 * (d // td)
    if latent:
        mod_idx = lambda b, j, i: (mod_base + b, 0, g2_blk + j)
    else:
        mod_idx = lambda b, j, i: (mod_base, 0, g2_blk + j)
    in_specs = [pl.BlockSpec((tn, td), lambda b, j, i: (b * npt + i, j)),
                pl.BlockSpec((tn, LANES), lambda b, j, i: (b * npt + i, 0)),
                pl.BlockSpec((tn, LANES), lambda b, j, i: (b * npt + i, 0)),
                pl.BlockSpec((ne, 1, cap, td), lambda b, j, i: (0, b, 0, j)),
                pl.BlockSpec((1, 1, td), mod_idx)]
    args = [x1, slot_t, gate_t, y4, mod3]
    k = ne * cap
    fused = k <= SCATTER_FUSED_K
    if fused:
        cols = np.arange(k)
        expand = (cols[None, :] // cap == np.arange(ne)[:, None]).astype(np.float32)
        in_specs += [pl.BlockSpec((ne, k), lambda b, j, i: (0, 0)), pl.BlockSpec((1, k), lambda b, j, i: (0, 0))]
        args += [jnp.asarray(expand, F32).astype(BF16), jnp.asarray((cols % cap)[None, :], F32)]
    return pl.pallas_call(
        functools.partial(_scatter_kernel, fused),
        out_shape=jax.ShapeDtypeStruct((t, d), F32),
        grid=(n_sets, d // td, npt),
        in_specs=in_specs,
        out_specs=pl.BlockSpec((tn, td), lambda b, j, i: (b * npt + i, j)),
        compiler_params=_cp(3),
        name="moe_scatter",
    )(*args)


def _mixer_ctx(x2d, batch, seq, mod3, lp):
    hy, qna, qg, nak, nav, gk, gv = _in_projection(
        x2d, mod3, 0, batch, seq, lp["norm1_g"], lp["w_in_bf"], lp["head_gains"], latent=False)
    y_hy = _hyena(hy.reshape(batch, seq, HY_IN), lp)
    ona, og = _ctx_attention(qna, qg, nak, nav, gk, gv)
    x1, h2, aff = _out_projection(x2d, y_hy, ona, og, mod3, 0, seq, lp["norm2_g"], lp["w_out_bf"],
                                  lp["rw_hi"], lp["rw_lo"], latent=False)
    return x1, h2, aff, (nak, nav, gk, gv)


def _mixer_lat(x2d, batch, seq, mod3, lp, caches, layer):
    hy, qna, kna, vna, qg, kg, vg = _in_projection(
        x2d, mod3, 1, batch, seq, lp["norm1_g"], lp["w_in_bf"], lp["head_gains"], latent=True)
    y_hy = _hyena(hy.reshape(batch, seq, HY_IN), lp)
    na_kc, na_vc, g_kc, g_vc = caches
    ona = _na_latent(qna, kna, vna, na_kc, na_vc, layer, lp["na_rpb"], batch, seq)
    og = _gqa_latent(qg, kg, vg, g_kc, g_vc, layer, batch, seq)
    x1, h2, aff = _out_projection(x2d, y_hy, ona, og, mod3, 1, seq, lp["norm2_g"], lp["w_out_bf"],
                                  lp["rw_hi"], lp["rw_lo"], latent=True)
    return x1, h2, aff


def _moe(parts, mod3, lp):
    outs = []
    for x1, h2, aff, n_sets, n, mod_base, latent in parts:
        slot, slot_t, gate_t = _route(aff, n_sets, n)
        xg = _gather(slot, h2, n_sets, n)
        ne, _, cap, d = xg.shape
        y = _expert_ffn(xg.reshape(ne, n_sets * cap, d),
                        lp["exp_w_gate"], lp["exp_w_up"], lp["exp_w_down"], lp["layer"])
        outs.append(_scatter_residual(x1, slot_t, gate_t, y.reshape(xg.shape), mod3, mod_base, n_sets, n,
                                      latent))
    return outs


def kernel(x_prompt, x_sample, cache_na_k, cache_na_v, cache_gqa_k, cache_gqa_v, c, c_ctx, ada_w, ada_b, norm1_g, norm2_g, w_in, w_out, hy_short_w, hy_short_b, hy_f_w1, hy_f_b1, hy_f_w2, hy_f_b2, hy_f_w3, hy_f_freq, hy_bias, na_q_g, na_k_g, na_rpb, gqa_q_g, gqa_k_g, router_w, exp_w_gate, exp_w_up, exp_w_down):
    batch, seq, d = x_prompt.shape
    dbatch, dseq, _ = x_sample.shape
    depth = ada_w.shape[0]
    rows = 8 * ((1 + dbatch + 7) // 8)
    cvec = jnp.zeros((rows, d), F32).at[0].set(c_ctx).at[1:1 + dbatch].set(c)
    mod_all = _modulation(cvec, ada_w, ada_b)
    yp = x_prompt.reshape(batch * seq, d)
    ys = x_sample.reshape(dbatch * dseq, d)
    rw_t = jnp.swapaxes(router_w, 1, 2)
    rw_hi = rw_t.astype(BF16)
    rw_lo = (rw_t - rw_hi.astype(F32)).astype(BF16)
    new_kv = [[], [], [], []]
    for l in range(depth):
        lp = {
            "norm1_g": norm1_g[l], "norm2_g": norm2_g[l],
            "w_in_bf": w_in[l].astype(BF16), "w_out_bf": w_out[l].astype(BF16),
            "hy_short_w": hy_short_w[l], "hy_short_b": hy_short_b[l],
            "hy_f_w1": hy_f_w1[l], "hy_f_b1": hy_f_b1[l], "hy_f_w2": hy_f_w2[l], "hy_f_b2": hy_f_b2[l],
            "hy_f_w3": hy_f_w3[l], "hy_f_freq": hy_f_freq[l], "hy_bias": hy_bias[l],
            "head_gains": jnp.stack([na_q_g[l], na_k_g[l], gqa_q_g[l], gqa_k_g[l]]),
            "na_rpb": na_rpb[l], "rw_hi": rw_hi[l], "rw_lo": rw_lo[l],
            "layer": l, "exp_w_gate": exp_w_gate, "exp_w_up": exp_w_up, "exp_w_down": exp_w_down,
        }
        mod3 = mod_all[l].reshape(rows, 1, 6 * d)
        xp1, hp2, affp, kv = _mixer_ctx(yp, batch, seq, mod3, lp)
        for dst, src in zip(new_kv, kv):
            dst.append(src)
        xs1, hs2, affs = _mixer_lat(ys, dbatch, dseq, mod3, lp,
                                    (cache_na_k, cache_na_v, cache_gqa_k, cache_gqa_v), l)
        yp, ys = _moe([(xp1, hp2, affp, batch, seq, 0, False), (xs1, hs2, affs, dbatch, dseq, 1, True)],
                      mod3, lp)
    outs = [jnp.stack(v, axis=1) for v in new_kv]
    return (yp.reshape(batch, seq, d), ys.reshape(dbatch, dseq, d), outs[0], outs[1], outs[2], outs[3])
```

```python
import functools
import math

import numpy as np
import jax
import jax.numpy as jnp
from jax import lax
from jax.experimental import pallas as pl
from jax.experimental.pallas import tpu as pltpu

F32 = jnp.float32
BF16 = jnp.bfloat16

HEAD_DIM = 128
GRID_W = 64
HY_WIDTH = 512
HY_ORDER = 2
HY_IN = (HY_ORDER + 1) * HY_WIDTH
HY_POS_EMB = 33
HY_DECAY_TARGET = 1e-2
HY_FAST_PCT = 0.3
HY_SLOW_PCT = 1.5
NA_HEADS = 6
NA_WIN_H = 8
NA_WIN_W = 16
GQA_Q_HEADS = 6
GQA_KV_HEADS = 2
GQA_GROUP = GQA_Q_HEADS // GQA_KV_HEADS
ROPE_THETA = 10000.0
N_EXPERTS = 16
EC_CAPACITY_FACTOR = 2
NORM_EPS = 1e-6
MASK_VALUE = -1e30
NA_W = NA_HEADS * HEAD_DIM
GQ_W = GQA_Q_HEADS * HEAD_DIM
GKV_W = GQA_KV_HEADS * HEAD_DIM

LANES = 128
VMEM_LIMIT_BYTES = 56 * 1024 * 1024
LOG2_E = math.log2(math.e)
DFT_ROWS = 16
DFT_INNER = 128
DIRECT_DFT_MAX_LEN = 512


def _cp(n_axes, vmem=VMEM_LIMIT_BYTES):
    return pltpu.CompilerParams(dimension_semantics=("arbitrary",) * n_axes, vmem_limit_bytes=vmem)


def _nt(a, b):
    return lax.dot_general(a, b, (((1,), (1,)), ((), ())), preferred_element_type=F32)


def _mm(a, b):
    return jnp.dot(a, b, preferred_element_type=F32)


def _mm_hi(a, b):
    return jnp.dot(a, b, preferred_element_type=F32, precision=lax.Precision.HIGHEST)


def _mod_kernel(c_ref, w_ref, b_ref, o_ref):
    c = c_ref[...]
    s = c * jax.nn.sigmoid(c)
    o_ref[0] = _mm(s.astype(BF16), w_ref[0].astype(BF16)) + b_ref[0]


def _modulation(cvec, ada_w, ada_b):
    depth, d, n = ada_w.shape
    rows = cvec.shape[0]
    tn = min(n, 512)
    return pl.pallas_call(
        _mod_kernel,
        out_shape=jax.ShapeDtypeStruct((depth, rows, n), F32),
        grid=(depth, n // tn),
        in_specs=[
            pl.BlockSpec((rows, d), lambda l, j: (0, 0)),
            pl.BlockSpec((1, d, tn), lambda l, j: (l, 0, j)),
            pl.BlockSpec((1, 1, tn), lambda l, j: (l, 0, j)),
        ],
        out_specs=pl.BlockSpec((1, rows, tn), lambda l, j: (l, 0, j)),
        compiler_params=_cp(2),
        name="modulation",
    )(cvec, ada_w, ada_b.reshape(depth, 1, n))


def _head_norm(t, g):
    return t * lax.rsqrt(jnp.mean(t * t, axis=-1, keepdims=True) + NORM_EPS) * g


def _rope(t, cos, sin):
    swapped = jnp.where((lax.broadcasted_iota(jnp.int32, t.shape, 1) % 64) < 32,
                        pltpu.roll(t, HEAD_DIM - 32, 1), pltpu.roll(t, 32, 1))
    return t * cos + swapped * sin


def _inproj_kernel(latent, x_ref, mod_ref, g1_ref, w_ref, hg_ref, *rest):
    if latent:
        cos_ref, sin_ref, hy_ref, qna_ref, kna_ref, vna_ref, qg_ref, kg_ref, vg_ref = rest
    else:
        hy_ref, qna_ref, qg_ref, nak_ref, nav_ref, gk_ref, gv_ref = rest
    d = x_ref.shape[-1]
    x = x_ref[...]
    mod = mod_ref[0]
    shift, scale = mod[:, 0:d], mod[:, d:2 * d]
    ms = jnp.mean(x * x, axis=-1, keepdims=True)
    h = (x * lax.rsqrt(ms + NORM_EPS) * g1_ref[...]) * (1.0 + scale) + shift
    hb = h.astype(BF16)

    def proj(c0, n):
        return _mm(hb, w_ref[:, c0:c0 + n])

    hy_ref[...] = proj(0, HY_IN)
    q_scale = HEAD_DIM ** -0.5 * LOG2_E
    g_naq, g_nak = hg_ref[0:1, :], hg_ref[1:2, :]
    g_gq, g_gk = hg_ref[2:3, :], hg_ref[3:4, :]
    c = HY_IN
    z_naq = proj(c, NA_W)
    z_nak = proj(c + NA_W, NA_W)
    z_nav = proj(c + 2 * NA_W, NA_W)
    c += 3 * NA_W
    z_gq = proj(c, GQ_W)
    z_gk = proj(c + GQ_W, GKV_W)
    z_gv = proj(c + GQ_W + GKV_W, GKV_W)
    if latent:
        cos, sin = cos_ref[...], sin_ref[...]
    seq = None if latent else nak_ref.shape[2]
    nb = None if latent else nak_ref.shape[0]
    for hd in range(NA_HEADS):
        sl = slice(hd * HEAD_DIM, (hd + 1) * HEAD_DIM)
        qna_ref[:, sl] = (_head_norm(z_naq[:, sl], g_naq) * q_scale).astype(BF16)
        k = _head_norm(z_nak[:, sl], g_nak)
        v = z_nav[:, sl]
        if latent:
            kna_ref[:, sl] = k.astype(BF16)
            vna_ref[:, sl] = v.astype(BF16)
        else:
            for b in range(nb):
                nak_ref[b, hd] = k[b * seq:(b + 1) * seq]
                nav_ref[b, hd] = v[b * seq:(b + 1) * seq]
    for hd in range(GQA_Q_HEADS):
        sl = slice(hd * HEAD_DIM, (hd + 1) * HEAD_DIM)
        q = _head_norm(z_gq[:, sl], g_gq)
        if latent:
            q = _rope(q, cos, sin)
        qg_ref[:, sl] = (q * q_scale).astype(BF16)
    for hd in range(GQA_KV_HEADS):
        sl = slice(hd * HEAD_DIM, (hd + 1) * HEAD_DIM)
        k = _head_norm(z_gk[:, sl], g_gk)
        v = z_gv[:, sl]
        if latent:
            kg_ref[:, sl] = _rope(k, cos, sin).astype(BF16)
            vg_ref[:, sl] = v.astype(BF16)
        else:
            for b in range(nb):
                gk_ref[b, hd] = k[b * seq:(b + 1) * seq]
                gv_ref[b, hd] = v[b * seq:(b + 1) * seq]


def _rope_tables(n):
    pos = np.arange(n)
    row = (pos // GRID_W).astype(np.float64)
    col = (pos % GRID_W).astype(np.float64)
    quarter = HEAD_DIM // 4
    inv = ROPE_THETA ** (-np.arange(quarter, dtype=np.float64) / quarter)
    ang_r = row[:, None] * inv[None, :]
    ang_c = col[:, None] * inv[None, :]
    cos = np.concatenate([np.cos(ang_r), np.cos(ang_r), np.cos(ang_c), np.cos(ang_c)], axis=-1)
    sin = np.concatenate([-np.sin(ang_r), np.sin(ang_r), -np.sin(ang_c), np.sin(ang_c)], axis=-1)
    return jnp.asarray(cos, F32), jnp.asarray(sin, F32)


def _in_projection(x2d, mod3, mod_base, batch, seq, g1, w_in_bf, head_gains, latent):
    t, d = x2d.shape
    tm = min(512, seq) if latent else min(512, t)
    if not latent:
        tm = max(tm, seq)
    nt_ = t // tm
    tiles_per_batch = seq // tm if latent else None
    wide = w_in_bf.shape[1]
    if latent:
        mod_idx = lambda i: (mod_base + i // tiles_per_batch, 0, 0)
    else:
        mod_idx = lambda i: (mod_base, 0, 0)
    in_specs = [
        pl.BlockSpec((tm, d), lambda i: (i, 0)),
        pl.BlockSpec((1, 1, mod3.shape[-1]), mod_idx),
        pl.BlockSpec((1, d), lambda i: (0, 0)),
        pl.BlockSpec((d, wide), lambda i: (0, 0), pipeline_mode=pl.Buffered(1)),
        pl.BlockSpec((4, HEAD_DIM), lambda i: (0, 0)),
    ]
    args = [x2d, mod3, g1.reshape(1, d), w_in_bf, head_gains]
    row_spec = lambda w: pl.BlockSpec((tm, w), lambda i: (i, 0))
    if latent:
        cos, sin = _rope_tables(seq)
        in_specs += [pl.BlockSpec((tm, HEAD_DIM), lambda i: (i % tiles_per_batch, 0))] * 2
        args += [cos, sin]
        out_shape = [
            jax.ShapeDtypeStruct((t, HY_IN), F32),
            jax.ShapeDtypeStruct((t, NA_W), BF16), jax.ShapeDtypeStruct((t, NA_W), BF16),
            jax.ShapeDtypeStruct((t, NA_W), BF16), jax.ShapeDtypeStruct((t, GQ_W), BF16),
            jax.ShapeDtypeStruct((t, GKV_W), BF16), jax.ShapeDtypeStruct((t, GKV_W), BF16),
        ]
        out_specs = [row_spec(HY_IN), row_spec(NA_W), row_spec(NA_W), row_spec(NA_W),
                     row_spec(GQ_W), row_spec(GKV_W), row_spec(GKV_W)]
    else:
        nb = tm // seq
        kv_spec = lambda hh: pl.BlockSpec((nb, hh, seq, HEAD_DIM), lambda i: (i, 0, 0, 0))
        out_shape = [
            jax.ShapeDtypeStruct((t, HY_IN), F32),
            jax.ShapeDtypeStruct((t, NA_W), BF16), jax.ShapeDtypeStruct((t, GQ_W), BF16),
            jax.ShapeDtypeStruct((batch, NA_HEADS, seq, HEAD_DIM), F32),
            jax.ShapeDtypeStruct((batch, NA_HEADS, seq, HEAD_DIM), F32),
            jax.ShapeDtypeStruct((batch, GQA_KV_HEADS, seq, HEAD_DIM), F32),
            jax.ShapeDtypeStruct((batch, GQA_KV_HEADS, seq, HEAD_DIM), F32),
        ]
        out_specs = [row_spec(HY_IN), row_spec(NA_W), row_spec(GQ_W),
                     kv_spec(NA_HEADS), kv_spec(NA_HEADS), kv_spec(GQA_KV_HEADS), kv_spec(GQA_KV_HEADS)]
    return pl.pallas_call(
        functools.partial(_inproj_kernel, latent),
        out_shape=out_shape,
        grid=(nt_,),
        in_specs=in_specs,
        out_specs=out_specs,
        compiler_params=_cp(1),
        name="in_proj_lat" if latent else "in_proj_ctx",
    )(*args)


def _shortconv_kernel(x_ref, w_ref, b_ref, o_ref):
    o_ref[0, 0] = _short_conv_rows(x_ref[0], w_ref, b_ref)


def _short_conv(hy3, sw, sb):
    b, l, _ = hy3.shape
    tc = 256
    per = HY_WIDTH // tc
    return pl.pallas_call(
        _shortconv_kernel,
        out_shape=jax.ShapeDtypeStruct((HY_ORDER + 1, b, l, HY_WIDTH), F32),
        grid=(b, HY_IN // tc),
        in_specs=[
            pl.BlockSpec((1, l, tc), lambda i, q: (i, 0, q)),
            pl.BlockSpec((3, tc), lambda i, q: (0, q)),
            pl.BlockSpec((1, tc), lambda i, q: (0, q)),
        ],
        out_specs=pl.BlockSpec((1, 1, l, tc), lambda i, q: (q // per, i, 0, q % per)),
        compiler_params=_cp(2),
        name="hy_short_conv",
    )(hy3, sw, sb.reshape(1, HY_IN))


def _filter_kernel(z_ref, w1_ref, b1_ref, w2_ref, b2_ref, w3_ref, fr_ref, dl_ref, f_ref, n_ref):
    i = pl.program_id(0)
    z = z_ref[...]
    sf = fr_ref[...]
    hid = jnp.sin(sf * (_mm_hi(z, w1_ref[...]) + b1_ref[...]))
    hid = jnp.sin(sf * (_mm_hi(hid, w2_ref[...]) + b2_ref[...]))
    filt = _mm_hi(hid, w3_ref[...])
    t = z[:, 0:1]
    decay = jnp.exp(-t * dl_ref[...])
    decay = jnp.concatenate([decay] * (2 * HY_ORDER), axis=-1)
    filt = filt * decay
    half = HY_ORDER * HY_WIDTH
    rows = lax.broadcasted_iota(jnp.int32, filt.shape, 0) + i * filt.shape[0]
    cols = lax.broadcasted_iota(jnp.int32, filt.shape, 1)
    filt = jnp.where((rows == 0) & (cols >= half), 0.0, filt)
    f_ref[...] = filt
    part = jnp.sum(jnp.abs(filt), axis=0, keepdims=True)

    @pl.when(i == 0)
    def _():
        n_ref[...] = jnp.zeros_like(n_ref)

    n_ref[...] += part[:, :half] + part[:, half:]


def _filter_embedding(l):
    t = np.linspace(0.0, 1.0, l, dtype=np.float64)[:, None]
    bands = (HY_POS_EMB - 1) // 2
    fr = np.linspace(1e-4, bands - 1, bands, dtype=np.float64)[None, :]
    w = 2.0 * math.pi * np.arange(l, dtype=np.float64)[:, None] / l
    z = np.concatenate([t, np.cos(fr * w), -np.sin(fr * w)], axis=-1)
    zp = np.zeros((l, LANES), np.float32)
    zp[:, :HY_POS_EMB] = z
    min_d = abs(math.log(HY_DECAY_TARGET) / HY_SLOW_PCT)
    max_d = abs(math.log(HY_DECAY_TARGET) / HY_FAST_PCT)
    deltas = np.linspace(min_d, max_d, HY_WIDTH, dtype=np.float64)[None, :]
    return jnp.asarray(zp, F32), jnp.asarray(deltas, F32)


def _hyena_filters(l, w1, b1, w2, b2, w3, freq):
    zp, deltas = _filter_embedding(l)
    hid = w1.shape[1]
    w1p = jnp.zeros((LANES, hid), F32).at[:HY_POS_EMB].set(w1)
    tl = min(l, 512)
    wide = w3.shape[1]
    full = lambda a: pl.BlockSpec(a.shape, lambda i: (0,) * a.ndim)
    b1r, b2r, frr = b1.reshape(1, hid), b2.reshape(1, hid), freq.reshape(1, hid)
    return pl.pallas_call(
        _filter_kernel,
        out_shape=[jax.ShapeDtypeStruct((l, wide), F32),
                   jax.ShapeDtypeStruct((1, HY_ORDER * HY_WIDTH), F32)],
        grid=(l // tl,),
        in_specs=[pl.BlockSpec((tl, LANES), lambda i: (i, 0)), full(w1p), full(b1r), full(w2), full(b2r),
                  full(w3), full(frr), full(deltas)],
        out_specs=[pl.BlockSpec((tl, wide), lambda i: (i, 0)),
                   pl.BlockSpec((1, HY_ORDER * HY_WIDTH), lambda i: (0, 0))],
        compiler_params=_cp(1),
        name="hy_filters",
    )(zp, w1p, b1r, w2, b2r, w3, frr, deltas)


def _bf16_const(a):
    return jnp.asarray(a, F32).astype(BF16)


def _dft_plan(l):
    n = 2 * l
    n2_len = 1 if l <= DIRECT_DFT_MAX_LEN else DFT_INNER
    n1_len = n // n2_len
    k1 = np.arange(n1_len, dtype=np.float64)[:, None]
    n1 = np.arange(n1_len // 2, dtype=np.float64)[None, :]
    ang = 2.0 * np.pi * k1 * n1 / n1_len
    plan = {
        "n1": n1_len, "n2": n2_len,
        "fs": _bf16_const(np.concatenate([np.cos(ang), -np.sin(ang)], axis=0)),
        "gr": _bf16_const(np.cos(ang).T / n),
        "gi": _bf16_const(-np.sin(ang).T / n),
    }
    if n2_len > 1:
        k2 = np.arange(n2_len, dtype=np.float64)
        a2 = 2.0 * np.pi * np.outer(k2, k2) / n2_len
        fr, fi = np.cos(a2), -np.sin(a2)
        plan["d2"] = _bf16_const(np.block([[fr, -fi], [fi, fr]]))
        plan["d2c"] = _bf16_const(np.block([[fr, fi], [-fi, fr]]))
        eye = np.eye(DFT_ROWS)
        plan["fsk"] = _bf16_const(np.kron(np.concatenate([np.cos(ang), -np.sin(ang)], axis=0), eye))
        plan["grk"] = _bf16_const(np.kron(np.cos(ang).T / n, eye))
        plan["gik"] = _bf16_const(np.kron(-np.sin(ang).T / n, eye))
        at = 2.0 * np.pi * np.arange(n1_len, dtype=np.float64)[:, None] * k2[None, :] / n
        plan["twr"] = jnp.asarray(np.repeat(np.cos(at)[:, :, None], LANES, axis=2), F32)
        plan["twi"] = jnp.asarray(np.repeat(-np.sin(at)[:, :, None], LANES, axis=2), F32)
    return plan


def _dft1_kernel(z_ref, fs_ref, ar_ref, ai_ref):
    a = _mm(fs_ref[...], z_ref[0, 0].astype(BF16))
    n1 = ar_ref.shape[1]
    ar_ref[0] = a[:n1].astype(BF16)
    ai_ref[0] = a[n1:].astype(BF16)


def _dft1_rows_kernel(z_ref, fsk_ref, ar_ref, ai_ref):
    _, _, kk, mb, c = z_ref.shape
    n1 = ar_ref.shape[1]
    a = _mm(fsk_ref[...], z_ref[0, 0].reshape(kk * mb, c).astype(BF16))
    ar_ref[0] = a[:n1 * mb].reshape(n1, mb, c).astype(BF16)
    ai_ref[0] = a[n1 * mb:].reshape(n1, mb, c).astype(BF16)


def _dft_stage1(z, part, plan):
    n1 = plan["n1"]
    if plan["n2"] > 1:
        _, b, kk, n2, c = z.shape
        mb = DFT_ROWS
        tc = min(c, 512)
        return pl.pallas_call(
            _dft1_rows_kernel,
            out_shape=[jax.ShapeDtypeStruct((b, n1, n2, c), BF16)] * 2,
            grid=(b, n2 // mb, c // tc),
            in_specs=[pl.BlockSpec((1, 1, kk, mb, tc), lambda i, j, q: (part, i, 0, j, q)),
                      pl.BlockSpec(plan["fsk"].shape, lambda i, j, q: (0, 0))],
            out_specs=[pl.BlockSpec((1, n1, mb, tc), lambda i, j, q: (i, 0, j, q))] * 2,
            compiler_params=_cp(3),
            name="hy_dft_outer",
        )(z, plan["fsk"])
    z4 = z
    _, b, kk, nc = z4.shape
    tn = min(nc, 4096)
    return pl.pallas_call(
        _dft1_kernel,
        out_shape=[jax.ShapeDtypeStruct((b, n1, nc), BF16)] * 2,
        grid=(b, nc // tn),
        in_specs=[pl.BlockSpec((1, 1, kk, tn), lambda i, j: (part, i, 0, j)),
                  pl.BlockSpec((2 * n1, kk), lambda i, j: (0, 0))],
        out_specs=[pl.BlockSpec((1, n1, tn), lambda i, j: (i, 0, j))] * 2,
        compiler_params=_cp(2),
        name="hy_dft_outer",
    )(z4, plan["fs"])


def _twiddle_inner_dft(ar, ai, twr, twi, d2):
    c = ar.shape[-1]
    twr = jnp.concatenate([twr] * (c // LANES), axis=-1)
    twi = jnp.concatenate([twi] * (c // LANES), axis=-1)
    zr = ar * twr - ai * twi
    zi = ar * twi + ai * twr
    x = _mm(d2, jnp.concatenate([zr, zi], axis=0).astype(BF16))
    n2 = ar.shape[0]
    return x[:n2], x[n2:], twr, twi


def _spec_filter_kernel(two_stage, kb, ar_ref, ai_ref, nrm_ref, *rest):
    if two_stage:
        twr_ref, twi_ref, d2_ref, kr_ref, ki_ref = rest
    else:
        kr_ref, ki_ref = rest
    half = HY_ORDER * HY_WIDTH
    inv = 1.0 / nrm_ref[...]

    def combine(xr, xi):
        return (xr[:, :half] + xr[:, half:]) * inv, (xi[:, :half] - xi[:, half:]) * inv

    if two_stage:
        for j in range(kb):
            xr, xi, _, _ = _twiddle_inner_dft(ar_ref[0, j].astype(F32), ai_ref[0, j].astype(F32),
                                              twr_ref[j], twi_ref[j], d2_ref[...])
            kr, ki = combine(xr, xi)
            for o in range(HY_ORDER):
                kr_ref[o, j] = kr[:, o * HY_WIDTH:(o + 1) * HY_WIDTH]
                ki_ref[o, j] = ki[:, o * HY_WIDTH:(o + 1) * HY_WIDTH]
    else:
        kr, ki = combine(ar_ref[0].astype(F32), ai_ref[0].astype(F32))
        for o in range(HY_ORDER):
            kr_ref[o] = kr[:, o * HY_WIDTH:(o + 1) * HY_WIDTH]
            ki_ref[o] = ki[:, o * HY_WIDTH:(o + 1) * HY_WIDTH]


def _filter_spectrum(filt, norm, plan):
    l, wide = filt.shape
    n1, n2 = plan["n1"], plan["n2"]
    two_stage = n2 > 1
    zf = filt.reshape(1, 1, n1 // 2, n2, wide) if two_stage else filt.reshape(1, 1, n1 // 2, wide)
    ar, ai = _dft_stage1(zf, 0, plan)
    if two_stage:
        kb = 8
        a_spec = pl.BlockSpec((1, kb, n2, wide), lambda i: (0, i, 0, 0))
        in_specs = [a_spec, a_spec, pl.BlockSpec(norm.shape, lambda i: (0, 0)),
                    pl.BlockSpec((kb, n2, LANES), lambda i: (i, 0, 0)),
                    pl.BlockSpec((kb, n2, LANES), lambda i: (i, 0, 0)),
                    pl.BlockSpec((2 * n2, 2 * n2), lambda i: (0, 0))]
        args = [ar, ai, norm, plan["twr"], plan["twi"], plan["d2"]]
        out_shape = [jax.ShapeDtypeStruct((HY_ORDER, n1, n2, HY_WIDTH), F32)] * 2
        out_specs = [pl.BlockSpec((HY_ORDER, kb, n2, HY_WIDTH), lambda i: (0, i, 0, 0))] * 2
    else:
        kb = min(n1, 256)
        a_spec = pl.BlockSpec((1, kb, wide), lambda i: (0, i, 0))
        in_specs = [a_spec, a_spec, pl.BlockSpec(norm.shape, lambda i: (0, 0))]
        args = [ar, ai, norm]
        out_shape = [jax.ShapeDtypeStruct((HY_ORDER, n1, HY_WIDTH), F32)] * 2
        out_specs = [pl.BlockSpec((HY_ORDER, kb, HY_WIDTH), lambda i: (0, i, 0))] * 2
    return pl.pallas_call(
        functools.partial(_spec_filter_kernel, two_stage, kb),
        out_shape=out_shape,
        grid=(n1 // kb,),
        in_specs=in_specs,
        out_specs=out_specs,
        compiler_params=_cp(1),
        name="hy_filter_spectrum",
    )(*args)


def _spec_mul_kernel(kb, ar_ref, ai_ref, kr_ref, ki_ref, twr_ref, twi_ref, d2_ref, d2c_ref, br_ref, bi_ref):
    for j in range(kb):
        xr, xi, twr, twi = _twiddle_inner_dft(ar_ref[0, j].astype(F32), ai_ref[0, j].astype(F32),
                                              twr_ref[j], twi_ref[j], d2_ref[...])
        kr, ki = kr_ref[0, j], ki_ref[0, j]
        yr = xr * kr - xi * ki
        yi = xr * ki + xi * kr
        bm = _mm(d2c_ref[...], jnp.concatenate([yr, yi], axis=0).astype(BF16))
        n2 = xr.shape[0]
        br, bi = bm[:n2], bm[n2:]
        br_ref[0, j] = (br * twr + bi * twi).astype(BF16)
        bi_ref[0, j] = (bi * twr - br * twi).astype(BF16)


def _spectrum_multiply(ar, ai, kr, ki, order, plan):
    b, n1, n2, c = ar.shape
    kb = 8
    a_spec = pl.BlockSpec((1, kb, n2, c), lambda j, i: (i, j, 0, 0))
    k_spec = pl.BlockSpec((1, kb, n2, c), lambda j, i: (order, j, 0, 0))
    tw_spec = pl.BlockSpec((kb, n2, LANES), lambda j, i: (j, 0, 0))
    d_spec = pl.BlockSpec((2 * n2, 2 * n2), lambda j, i: (0, 0))
    return pl.pallas_call(
        functools.partial(_spec_mul_kernel, kb),
        out_shape=[jax.ShapeDtypeStruct((b, n1, n2, c), BF16)] * 2,
        grid=(n1 // kb, b),
        in_specs=[a_spec, a_spec, k_spec, k_spec, tw_spec, tw_spec, d_spec, d_spec],
        out_specs=[a_spec, a_spec],
        compiler_params=_cp(2),
        name="hy_spectrum_mul",
    )(ar, ai, kr, ki, plan["twr"], plan["twi"], plan["d2"], plan["d2c"])


def _idft1_rows_kernel(br_ref, bi_ref, grk_ref, gik_ref, z_ref, x_ref, bias_ref, o_ref):
    _, n1, mb, c = br_ref.shape
    conv = (_mm(grk_ref[...], br_ref[0].reshape(n1 * mb, c))
            + _mm(gik_ref[...], bi_ref[0].reshape(n1 * mb, c)))
    conv = conv.reshape(n1 // 2, mb, c)
    z = z_ref[0, 0]
    o_ref[0, 0] = (x_ref[0, 0] * (conv + z * bias_ref[...])).astype(o_ref.dtype)


def _idft_gate(br, bi, plan, z5, z_part, u5, x_part, bias_row, out_dtype):
    b, n1, n2, c = br.shape
    kk = n1 // 2
    mb = DFT_ROWS
    b_spec = pl.BlockSpec((1, n1, mb, c), lambda i, j: (i, 0, j, 0))
    g_spec = pl.BlockSpec(plan["grk"].shape, lambda i, j: (0, 0))
    return pl.pallas_call(
        _idft1_rows_kernel,
        out_shape=jax.ShapeDtypeStruct((1, b, kk, n2, c), out_dtype),
        grid=(b, n2 // mb),
        in_specs=[b_spec, b_spec, g_spec, g_spec,
                  pl.BlockSpec((1, 1, kk, mb, c), lambda i, j: (z_part, i, 0, j, 0)),
                  pl.BlockSpec((1, 1, kk, mb, c), lambda i, j: (x_part, i, 0, j, 0)),
                  pl.BlockSpec((1, c), lambda i, j: (0, 0))],
        out_specs=pl.BlockSpec((1, 1, kk, mb, c), lambda i, j: (0, i, 0, j, 0)),
        compiler_params=_cp(2),
        name="hy_idft_gate",
    )(br, bi, plan["grk"], plan["gik"], z5, u5, bias_row)


def _short_conv_rows(x, w_ref, b_ref):
    n = x.shape[0]
    rows = lax.broadcasted_iota(jnp.int32, x.shape, 0)
    prev = jnp.where(rows == 0, 0.0, pltpu.roll(x, 1, 0))
    nxt = jnp.where(rows == n - 1, 0.0, pltpu.roll(x, n - 1, 0))
    return prev * w_ref[0:1, :] + x * w_ref[1:2, :] + nxt * w_ref[2:3, :] + b_ref[...]


def _hyena_direct_kernel(hy_ref, sw_ref, sb_ref, fs_ref, gr_ref, gi_ref, kr_ref, ki_ref, hb_ref, o_ref):
    u = _short_conv_rows(hy_ref[0], sw_ref, sb_ref)
    nfreq = kr_ref.shape[1]
    z = u[:, 0:HY_WIDTH]
    for o in range(HY_ORDER):
        a = _mm(fs_ref[...], z.astype(BF16))
        ar, ai = a[:nfreq], a[nfreq:]
        kr, ki = kr_ref[o], ki_ref[o]
        br = (ar * kr - ai * ki).astype(BF16)
        bi = (ar * ki + ai * kr).astype(BF16)
        conv = _mm(gr_ref[...], br) + _mm(gi_ref[...], bi)
        z = u[:, (o + 1) * HY_WIDTH:(o + 2) * HY_WIDTH] * (conv + z * hb_ref[o:o + 1, :])
    o_ref[...] = z.astype(o_ref.dtype)


def _hyena_direct(hy3, lp, plan, kr, ki):
    b, l, wide = hy3.shape
    n1 = plan["n1"]
    full = lambda a: pl.BlockSpec(a.shape, lambda i: (0,) * a.ndim)
    sb = lp["hy_short_b"].reshape(1, wide)
    args = [lp["hy_short_w"], sb, plan["fs"], plan["gr"], plan["gi"], kr, ki, lp["hy_bias"]]
    return pl.pallas_call(
        _hyena_direct_kernel,
        out_shape=jax.ShapeDtypeStruct((b * l, HY_WIDTH), BF16),
        grid=(b,),
        in_specs=[pl.BlockSpec((1, l, wide), lambda i: (i, 0, 0))] + [full(a) for a in args],
        out_specs=pl.BlockSpec((l, HY_WIDTH), lambda i: (i, 0)),
        compiler_params=_cp(1),
        name="hy_direct",
    )(hy3, *args)


def _hyena(hy3, lp):
    b, l, _ = hy3.shape
    plan = _dft_plan(l)
    n1, n2 = plan["n1"], plan["n2"]
    filt, norm = _hyena_filters(l, lp["hy_f_w1"], lp["hy_f_b1"], lp["hy_f_w2"], lp["hy_f_b2"],
                                lp["hy_f_w3"], lp["hy_f_freq"])
    kr, ki = _filter_spectrum(filt, norm, plan)
    if n2 == 1:
        return _hyena_direct(hy3, lp, plan, kr, ki)
    u = _short_conv(hy3, lp["hy_short_w"], lp["hy_short_b"])
    u5 = u.reshape(HY_ORDER + 1, b, n1 // 2, n2, HY_WIDTH)
    z5, z_part = u5, 0
    for o in range(HY_ORDER):
        ar, ai = _dft_stage1(z5, z_part, plan)
        br, bi = _spectrum_multiply(ar, ai, kr, ki, o, plan)
        bias_row = lp["hy_bias"][o].reshape(1, HY_WIDTH)
        last = o == HY_ORDER - 1
        z5 = _idft_gate(br, bi, plan, z5, z_part, u5, o + 1, bias_row, BF16 if last else F32)
        z_part = 0
    return z5.reshape(b * l, HY_WIDTH)


def _softmax_pv(s_list, v_list):
    m = s_list[0].max(axis=-1, keepdims=True)
    for s in s_list[1:]:
        m = jnp.maximum(m, s.max(axis=-1, keepdims=True))
    den = 0.0
    o = 0.0
    for s, v in zip(s_list, v_list):
        e = jnp.exp2(s - m)
        den = den + e.sum(axis=-1, keepdims=True)
        o = o + _mm(e.astype(BF16), v)
    return o / den


def _ctx_attn_kernel(qna_ref, qg_ref, nak_ref, nav_ref, gk_ref, gv_ref, ona_ref, og_ref):
    for hd in range(NA_HEADS):
        sl = slice(hd * HEAD_DIM, (hd + 1) * HEAD_DIM)
        k = nak_ref[0, hd].astype(BF16)
        v = nav_ref[0, hd].astype(BF16)
        ona_ref[:, sl] = _softmax_pv([_nt(qna_ref[:, sl], k)], [v]).astype(BF16)
    for g in range(GQA_KV_HEADS):
        k = gk_ref[0, g].astype(BF16)
        v = gv_ref[0, g].astype(BF16)
        for r in range(GQA_GROUP):
            hd = g * GQA_GROUP + r
            sl = slice(hd * HEAD_DIM, (hd + 1) * HEAD_DIM)
            og_ref[:, sl] = _softmax_pv([_nt(qg_ref[:, sl], k)], [v]).astype(BF16)


def _ctx_attention(qna, qg, nak, nav, gk, gv):
    batch, _, seq, _ = nak.shape
    t = batch * seq
    kv = lambda hh: pl.BlockSpec((1, hh, seq, HEAD_DIM), lambda i: (i, 0, 0, 0))
    row = lambda w: pl.BlockSpec((seq, w), lambda i: (i, 0))
    return pl.pallas_call(
        _ctx_attn_kernel,
        out_shape=[jax.ShapeDtypeStruct((t, NA_W), BF16), jax.ShapeDtypeStruct((t, GQ_W), BF16)],
        grid=(batch,),
        in_specs=[row(NA_W), row(GQ_W), kv(NA_HEADS), kv(NA_HEADS), kv(GQA_KV_HEADS), kv(GQA_KV_HEADS)],
        out_specs=[row(NA_W), row(GQ_W)],
        compiler_params=_cp(1),
        name="ctx_attention",
    )(qna, qg, nak, nav, gk, gv)


NA_Q_ROWS = 2


def _na_tables(rows):
    r_q = NA_Q_ROWS
    kh = min(NA_WIN_H, rows)
    win = min(r_q + kh, rows)
    nblk = rows // r_q
    starts, var_ids, variants, keys = [], [], [], {}
    qr = np.arange(r_q)[:, None]
    kr = np.arange(win)[None, :]
    for j in range(nblk):
        start = int(np.clip(r_q * j - kh // 2, 0, rows - win))
        r = r_q * j + qr
        rs = np.clip(r - kh // 2, 0, rows - kh)
        kabs = start + kr
        row_ok = (kabs >= rs) & (kabs < rs + kh)
        ridx = np.where(row_ok, kabs - r + NA_WIN_H - 1, 0)
        key = ridx.tobytes() + row_ok.tobytes()
        if key not in keys:
            keys[key] = len(variants)
            variants.append((ridx, row_ok))
        starts.append(start)
        var_ids.append(keys[key])
    ridx = np.stack([v[0] for v in variants])
    row_ok = np.stack([v[1] for v in variants])
    qc = np.arange(GRID_W)[:, None]
    kc = np.arange(GRID_W)[None, :]
    cstart = np.clip(qc - NA_WIN_W // 2, 0, GRID_W - NA_WIN_W)
    col_ok = (kc >= cstart) & (kc < cstart + NA_WIN_W)
    return win, np.asarray(starts, np.int32), np.asarray(var_ids, np.int32), ridx, row_ok, col_ok


def _na_bias(rpb, ridx, row_ok, col_ok):
    h, nd, nrel = rpb.shape
    half = NA_WIN_W - 1
    period = 2 * GRID_W
    w = jnp.concatenate([rpb[..., half:], jnp.zeros((h, nd, period - nrel), F32), rpb[..., :half]], axis=-1)
    toep = jnp.tile(w, (1, 1, GRID_W))[..., :GRID_W * (period - 1)]
    toep = toep.reshape(h, nd, GRID_W, period - 1)[..., :GRID_W]
    toep = jnp.where(jnp.asarray(col_ok), toep * LOG2_E, MASK_VALUE)
    nvar, r_q, win = ridx.shape
    blocks = jnp.take(toep, jnp.asarray(ridx.reshape(-1)), axis=1)
    blocks = blocks.reshape(h, nvar, r_q, win, GRID_W, GRID_W)
    blocks = jnp.where(jnp.asarray(row_ok)[None, :, :, :, None, None], blocks, MASK_VALUE)
    return blocks.transpose(0, 1, 2, 4, 3, 5).reshape(h, nvar, r_q * GRID_W, win * GRID_W)


def _na_lat_kernel(win, start_ref, var_ref, q_ref, k_ref, v_ref, kc_ref, vc_ref, bias_ref, o_ref):
    j = pl.program_id(1)
    off = pl.multiple_of(start_ref[j] * GRID_W, GRID_W)
    for hd in range(NA_HEADS):
        sl = slice(hd * HEAD_DIM, (hd + 1) * HEAD_DIM)
        q = q_ref[:, sl]
        kw = k_ref[pl.ds(off, win * GRID_W), sl]
        vw = v_ref[pl.ds(off, win * GRID_W), sl]
        s_loc = _nt(q, kw) + bias_ref[hd, 0]
        s_ctx = _nt(q, kc_ref[0, 0, hd].astype(BF16))
        o_ref[:, sl] = _softmax_pv([s_loc, s_ctx], [vw, vc_ref[0, 0, hd].astype(BF16)]).astype(BF16)


def _na_latent(q, k, v, kc, vc, layer, rpb, batch, n):
    rows = n // GRID_W
    win, starts, var_ids, ridx, row_ok, col_ok = _na_tables(rows)
    bias = _na_bias(rpb, ridx, row_ok, col_ok)
    r_q = NA_Q_ROWS
    nblk = rows // r_q
    tq = r_q * GRID_W
    wk = win * GRID_W
    past = kc.shape[3]
    grid_spec = pltpu.PrefetchScalarGridSpec(
        num_scalar_prefetch=2,
        grid=(batch, nblk),
        in_specs=[
            pl.BlockSpec((tq, NA_W), lambda b, j, st, vr: (b * nblk + j, 0)),
            pl.BlockSpec((n, NA_W), lambda b, j, st, vr: (b, 0)),
            pl.BlockSpec((n, NA_W), lambda b, j, st, vr: (b, 0)),
            pl.BlockSpec((1, 1, NA_HEADS, past, HEAD_DIM), lambda b, j, st, vr: (b, layer, 0, 0, 0)),
            pl.BlockSpec((1, 1, NA_HEADS, past, HEAD_DIM), lambda b, j, st, vr: (b, layer, 0, 0, 0)),
            pl.BlockSpec((NA_HEADS, 1, tq, wk), lambda b, j, st, vr: (0, vr[j], 0, 0)),
        ],
        out_specs=pl.BlockSpec((tq, NA_W), lambda b, j, st, vr: (b * nblk + j, 0)),
    )
    return pl.pallas_call(
        functools.partial(_na_lat_kernel, win),
        out_shape=jax.ShapeDtypeStruct((batch * n, NA_W), BF16),
        grid_spec=grid_spec,
        compiler_params=_cp(2),
        name="na_latent",
    )(jnp.asarray(starts), jnp.asarray(var_ids), q, k, v, kc, vc, bias)


GQA_KEY_CHUNK = 512


def _gqa_lat_kernel(q_ref, k_ref, v_ref, kc_ref, vc_ref, o_ref, kall_ref, vall_ref, m_ref, acc_ref):
    tq = q_ref.shape[0]
    n = k_ref.shape[0]
    past = kc_ref.shape[3]
    total = n + past

    @pl.when(pl.program_id(2) == 0)
    def _():
        kall_ref[0:n, :] = k_ref[...]
        kall_ref[n:total, :] = kc_ref[0, 0, 0].astype(BF16)
        vall_ref[0:n, 0:HEAD_DIM] = v_ref[...]
        vall_ref[n:total, 0:HEAD_DIM] = vc_ref[0, 0, 0].astype(BF16)
        vall_ref[:, HEAD_DIM:2 * HEAD_DIM] = jnp.ones((total, HEAD_DIM), BF16)

    q = jnp.concatenate([q_ref[:, r * HEAD_DIM:(r + 1) * HEAD_DIM] for r in range(GQA_GROUP)], axis=0)
    m_ref[...] = jnp.full(m_ref.shape, -jnp.inf, F32)
    acc_ref[...] = jnp.zeros(acc_ref.shape, F32)

    def update(off, size):
        s = _nt(q, kall_ref[pl.ds(off, size), :])
        m_old = m_ref[...]
        m_new = jnp.maximum(m_old, s.max(axis=-1, keepdims=True))
        alpha = jnp.exp2(m_old - m_new)
        e = jnp.exp2(s - jnp.concatenate([m_new] * (size // LANES), axis=1))
        acc_ref[...] = (jnp.concatenate([alpha, alpha], axis=1) * acc_ref[...]
                        + _mm(e.astype(BF16), vall_ref[pl.ds(off, size), :]))
        m_ref[...] = m_new

    tk = min(GQA_KEY_CHUNK, total)
    full = total // tk

    def body(c, carry):
        update(pl.multiple_of(c * tk, tk), tk)
        return carry

    lax.fori_loop(0, full, body, 0, unroll=True)
    if total % tk:
        update(full * tk, total % tk)
    acc = acc_ref[...]
    o = acc[:, 0:HEAD_DIM] / acc[:, HEAD_DIM:2 * HEAD_DIM]
    for r in range(GQA_GROUP):
        o_ref[:, r * HEAD_DIM:(r + 1) * HEAD_DIM] = o[r * tq:(r + 1) * tq].astype(BF16)


def _gqa_latent(q, k, v, kc, vc, layer, batch, n):
    tq = min(256, n)
    nq = n // tq
    past = kc.shape[3]
    gw = GQA_GROUP * HEAD_DIM
    return pl.pallas_call(
        _gqa_lat_kernel,
        out_shape=jax.ShapeDtypeStruct((batch * n, GQ_W), BF16),
        grid=(batch, GQA_KV_HEADS, nq),
        in_specs=[
            pl.BlockSpec((tq, gw), lambda b, g, i: (b * nq + i, g)),
            pl.BlockSpec((n, HEAD_DIM), lambda b, g, i: (b, g)),
            pl.BlockSpec((n, HEAD_DIM), lambda b, g, i: (b, g)),
            pl.BlockSpec((1, 1, 1, past, HEAD_DIM), lambda b, g, i: (b, layer, g, 0, 0)),
            pl.BlockSpec((1, 1, 1, past, HEAD_DIM), lambda b, g, i: (b, layer, g, 0, 0)),
        ],
        out_specs=pl.BlockSpec((tq, gw), lambda b, g, i: (b * nq + i, g)),
        scratch_shapes=[pltpu.VMEM((n + past, HEAD_DIM), BF16), pltpu.VMEM((n + past, 2 * HEAD_DIM), BF16),
                        pltpu.VMEM((GQA_GROUP * tq, LANES), F32),
                        pltpu.VMEM((GQA_GROUP * tq, 2 * HEAD_DIM), F32)],
        compiler_params=_cp(3),
        name="gqa_latent",
    )(q, k, v, kc, vc)


def _outproj_kernel(x_ref, hy_ref, ona_ref, og_ref, mod_ref, g2_ref, wo_ref, rwh_ref, rwl_ref,
                    x1_ref, h2_ref, aff_ref):
    d = x_ref.shape[-1]
    mod = mod_ref[0]
    gate1 = mod[:, 2 * d:3 * d]
    shift2, scale2 = mod[:, 3 * d:4 * d], mod[:, 4 * d:5 * d]
    c1 = HY_WIDTH
    c2 = HY_WIDTH + NA_W
    c3 = c2 + GQ_W
    mixed = (_mm(hy_ref[...], wo_ref[0:c1, :]) + _mm(ona_ref[...], wo_ref[c1:c2, :])
             + _mm(og_ref[...], wo_ref[c2:c3, :]))
    x1 = x_ref[...] + gate1 * mixed
    x1_ref[...] = x1
    ms = jnp.mean(x1 * x1, axis=-1, keepdims=True)
    h2 = (x1 * lax.rsqrt(ms + NORM_EPS) * g2_ref[...]) * (1.0 + scale2) + shift2
    h2h = h2.astype(BF16)
    h2_ref[...] = h2h
    h2l = (h2 - h2h.astype(F32)).astype(BF16)
    logits = _nt(rwh_ref[...], h2h) + _nt(rwh_ref[...], h2l) + _nt(rwl_ref[...], h2h)
    m = logits.max(axis=0, keepdims=True)
    e = jnp.exp(logits - m)
    aff_ref[...] = e / e.sum(axis=0, keepdims=True)


def _out_projection(x2d, hy, ona, og, mod3, mod_base, seq, g2, w_out_bf, rw_hi, rw_lo, latent):
    t, d = x2d.shape
    tm = min(512, seq) if latent else min(512, t)
    tiles_per_batch = max(seq // tm, 1)
    if latent:
        mod_idx = lambda i: (mod_base + i // tiles_per_batch, 0, 0)
    else:
        mod_idx = lambda i: (mod_base, 0, 0)
    row = lambda w: pl.BlockSpec((tm, w), lambda i: (i, 0))
    ne = rw_hi.shape[0]
    return pl.pallas_call(
        _outproj_kernel,
        out_shape=[jax.ShapeDtypeStruct((t, d), F32), jax.ShapeDtypeStruct((t, d), BF16),
                   jax.ShapeDtypeStruct((ne, t), F32)],
        grid=(t // tm,),
        in_specs=[row(d), row(HY_WIDTH), row(NA_W), row(GQ_W),
                  pl.BlockSpec((1, 1, mod3.shape[-1]), mod_idx),
                  pl.BlockSpec((1, d), lambda i: (0, 0)),
                  pl.BlockSpec(w_out_bf.shape, lambda i: (0, 0), pipeline_mode=pl.Buffered(1)),
                  pl.BlockSpec((ne, d), lambda i: (0, 0)),
                  pl.BlockSpec((ne, d), lambda i: (0, 0))],
        out_specs=[row(d), row(d), pl.BlockSpec((ne, tm), lambda i: (0, i))],
        compiler_params=_cp(1),
        name="out_proj_lat" if latent else "out_proj_ctx",
    )(x2d, hy, ona, og, mod3, g2.reshape(1, d), w_out_bf, rw_hi, rw_lo)


ROUTE_CHUNK = 256


def _prefix_count(mask_f, tri):
    ne, n = mask_f.shape
    ch = tri.shape[0]
    pieces = []
    carry = jnp.zeros((ne, 1), F32)
    for c in range(n // ch):
        mk = mask_f[:, c * ch:(c + 1) * ch]
        inc = _mm(mk.astype(BF16), tri)
        pieces.append(inc - mk + carry)
        carry = carry + inc[:, ch - 1:ch]
    return (pieces[0] if len(pieces) == 1 else jnp.concatenate(pieces, axis=1)), carry


def _route_kernel(cap, n, aff_ref, slot_ref, slott_ref, gatet_ref):
    ne = aff_ref.shape[0]
    sets = aff_ref.shape[1] // n
    affs = [aff_ref[:, s * n:(s + 1) * n] for s in range(sets)]

    def bisect(i, thr_bits):
        out = []
        for aff, bits in zip(affs, thr_bits):
            cand = bits | (jnp.int32(1) << (30 - i))
            cnt = jnp.sum(jnp.where(aff >= pltpu.bitcast(cand, F32), 1.0, 0.0), axis=1, keepdims=True)
            out.append(jnp.where(cnt >= cap, cand, bits))
        return tuple(out)

    thr_bits = lax.fori_loop(0, 31, bisect, tuple(jnp.zeros((ne, 1), jnp.int32) for _ in range(sets)))
    ch = min(ROUTE_CHUNK, n)
    tri = jnp.where(lax.broadcasted_iota(jnp.int32, (ch, ch), 0) <= lax.broadcasted_iota(jnp.int32, (ch, ch), 1),
                    1.0, 0.0).astype(BF16)
    pad = jnp.zeros((LANES - ne, ch), F32)
    for s, aff in enumerate(affs):
        thr = pltpu.bitcast(thr_bits[s], F32)
        gt = jnp.where(aff > thr, 1.0, 0.0)
        eq = jnp.where(aff == thr, 1.0, 0.0)
        n_gt = jnp.sum(gt, axis=1, keepdims=True)
        eq_rank, _ = _prefix_count(eq, tri)
        sel = gt + eq * (eq_rank < (cap - n_gt)).astype(F32)
        rank, _ = _prefix_count(sel, tri)
        slot = jnp.where(sel > 0.0, rank, -1.0).astype(jnp.int32)
        slot_ref[:, s * n:(s + 1) * n] = slot
        for c in range(n // ch):
            r0 = s * n + c * ch
            blk = jnp.concatenate([slot[:, c * ch:(c + 1) * ch].astype(F32), pad], axis=0)
            slott_ref[r0:r0 + ch, :] = blk.T.astype(jnp.int32)
            gblk = jnp.concatenate([aff[:, c * ch:(c + 1) * ch], pad], axis=0)
            gatet_ref[r0:r0 + ch, :] = gblk.T


ROUTE_BLOCK_TOKENS = 2048


def _route(aff_t, n_sets, n):
    ne, t = aff_t.shape
    cap = EC_CAPACITY_FACTOR * n // N_EXPERTS
    sets = min(n_sets, max(1, ROUTE_BLOCK_TOKENS // n))
    blk = sets * n
    return pl.pallas_call(
        functools.partial(_route_kernel, cap, n),
        out_shape=[jax.ShapeDtypeStruct((ne, t), jnp.int32), jax.ShapeDtypeStruct((t, LANES), jnp.int32),
                   jax.ShapeDtypeStruct((t, LANES), F32)],
        grid=(n_sets // sets,),
        in_specs=[pl.BlockSpec((ne, blk), lambda b: (0, b))],
        out_specs=[pl.BlockSpec((ne, blk), lambda b: (0, b)), pl.BlockSpec((blk, LANES), lambda b: (b, 0)),
                   pl.BlockSpec((blk, LANES), lambda b: (b, 0))],
        compiler_params=_cp(1),
        name="route",
    )(aff_t)


GATHER_ROWS = 512
MOE_SMALL_SET = 512


def _gather_kernel(slot_ref, h_ref, o_ref):
    eb, _, cap, _ = o_ref.shape
    n = slot_ref.shape[-1]
    srow = lax.broadcasted_iota(jnp.int32, (cap, n), 0)
    onehot = [jnp.where(slot_ref[e] == srow, 1.0, 0.0).astype(BF16) for e in range(eb)]
    onehot = onehot[0] if eb == 1 else jnp.concatenate(onehot, axis=0)
    xg = _mm(onehot, h_ref[...]).astype(BF16)
    for e in range(eb):
        o_ref[e, 0] = xg[e * cap:(e + 1) * cap]


def _gather(slot, h2, n_sets, n):
    ne, t = slot.shape
    d = h2.shape[1]
    cap = EC_CAPACITY_FACTOR * n // N_EXPERTS
    eb = min(ne, max(1, GATHER_ROWS // cap))
    td = d if n <= MOE_SMALL_SET else min(d, 512)
    slot3 = slot.reshape(ne, 1, t)
    return pl.pallas_call(
        _gather_kernel,
        out_shape=jax.ShapeDtypeStruct((ne, n_sets, cap, d), BF16),
        grid=(n_sets, d // td, ne // eb),
        in_specs=[pl.BlockSpec((eb, 1, n), lambda b, j, e: (e, 0, b)),
                  pl.BlockSpec((n, td), lambda b, j, e: (b, j))],
        out_specs=pl.BlockSpec((eb, 1, cap, td), lambda b, j, e: (e, b, 0, j)),
        compiler_params=_cp(3),
        name="moe_gather",
    )(slot3, h2)


def _ffn_kernel(x_ref, wg_ref, wu_ref, wd_ref, y_ref, acc_ref):
    f = pl.program_id(2)

    @pl.when(f == 0)
    def _():
        acc_ref[...] = jnp.zeros_like(acc_ref)

    x = x_ref[0]
    a = _mm(x, wg_ref[0, 0].astype(BF16))
    u = _mm(x, wu_ref[0, 0].astype(BF16))
    hmid = (a * jax.nn.sigmoid(a) * u).astype(BF16)
    acc_ref[...] += _mm(hmid, wd_ref[0, 0].astype(BF16))

    @pl.when(f == pl.num_programs(2) - 1)
    def _():
        y_ref[0] = acc_ref[...].astype(BF16)


FFN_ROWS = 1024
FFN_COLS = 512


def _expert_ffn(xg, w_gate, w_up, w_down, layer):
    ne, m, d = xg.shape
    ff = w_gate.shape[-1]
    tf = min(ff, FFN_COLS)
    tm = min(m, FFN_ROWS)
    return pl.pallas_call(
        _ffn_kernel,
        out_shape=jax.ShapeDtypeStruct((ne, m, d), BF16),
        grid=(ne, m // tm, ff // tf),
        in_specs=[pl.BlockSpec((1, tm, d), lambda e, i, f: (e, i, 0)),
                  pl.BlockSpec((1, 1, d, tf), lambda e, i, f: (layer, e, 0, f)),
                  pl.BlockSpec((1, 1, d, tf), lambda e, i, f: (layer, e, 0, f)),
                  pl.BlockSpec((1, 1, tf, d), lambda e, i, f: (layer, e, f, 0))],
        out_specs=pl.BlockSpec((1, tm, d), lambda e, i, f: (e, i, 0)),
        scratch_shapes=[pltpu.VMEM((tm, d), F32)],
        compiler_params=_cp(3),
        name="moe_ffn",
    )(xg, w_gate, w_up, w_down)


def _scatter_kernel(fused, x_ref, slott_ref, gatet_ref, y_ref, gate2_ref, *rest):
    td = x_ref.shape[-1]
    ne, _, cap, _ = y_ref.shape
    tn = x_ref.shape[0]
    st = slott_ref[:, 0:ne]
    gt = gatet_ref[:, 0:ne]
    if fused:
        expand_ref, target_ref, o_ref = rest
        expand = expand_ref[...]
        slot_k = _mm(st.astype(F32).astype(BF16), expand)
        hit = slot_k == target_ref[...]
        g_hi = gt.astype(BF16)
        g_lo = (gt - g_hi.astype(F32)).astype(BF16)
        y_all = jnp.concatenate([y_ref[e, 0] for e in range(ne)], axis=0)
        moe = (_mm(jnp.where(hit, _mm(g_hi, expand), 0.0).astype(BF16), y_all)
               + _mm(jnp.where(hit, _mm(g_lo, expand), 0.0).astype(BF16), y_all))
    else:
        (o_ref,) = rest
        lane = lax.broadcasted_iota(jnp.int32, (tn, cap), 1)
        moe = jnp.zeros((tn, td), F32)
        for e in range(ne):
            onehot = jnp.where(st[:, e:e + 1] == lane, 1.0, 0.0).astype(BF16)
            moe = moe + gt[:, e:e + 1] * _mm(onehot, y_ref[e, 0])
    o_ref[...] = x_ref[...] + gate2_ref[0] * moe


SCATTER_FUSED_K = 512


def _scatter_residual(x1, slot_t, gate_t, y4, mod3, mod_base, n_sets, n, latent):
    t, d = x1.shape
    ne, _, cap, _ = y4.shape
    tn = min(n, 256)
    td = d if n <= MOE_SMALL_SET else min(d, 512)
    npt = n // tn
    g2_blk = 5 * (d // td)
    if latent:
        mod_idx = lambda b, j, i: (mod_base + b, 0, g2_blk + j)
    else:
        mod_idx = lambda b, j, i: (mod_base, 0, g2_blk + j)
    in_specs = [pl.BlockSpec((tn, td), lambda b, j, i: (b * npt + i, j)),
                pl.BlockSpec((tn, LANES), lambda b, j, i: (b * npt + i, 0)),
                pl.BlockSpec((tn, LANES), lambda b, j, i: (b * npt + i, 0)),
                pl.BlockSpec((ne, 1, cap, td), lambda b, j, i: (0, b, 0, j)),
                pl.BlockSpec((1, 1, td), mod_idx)]
    args = [x1, slot_t, gate_t, y4, mod3]
    k = ne * cap
    fused = k <= SCATTER_FUSED_K
    if fused:
        cols = np.arange(k)
        expand = (cols[None, :] // cap == np.arange(ne)[:, None]).astype(np.float32)
        in_specs += [pl.BlockSpec((ne, k), lambda b, j, i: (0, 0)), pl.BlockSpec((1, k), lambda b, j, i: (0, 0))]
        args += [jnp.asarray(expand, F32).astype(BF16), jnp.asarray((cols % cap)[None, :], F32)]
    return pl.pallas_call(
        functools.partial(_scatter_kernel, fused),
        out_shape=jax.ShapeDtypeStruct((t, d), F32),
        grid=(n_sets, d // td, npt),
        in_specs=in_specs,
        out_specs=pl.BlockSpec((tn, td), lambda b, j, i: (b * npt + i, j)),
        compiler_params=_cp(3),
        name="moe_scatter",
    )(*args)


def _mixer_ctx(x2d, batch, seq, mod3, lp):
    hy, qna, qg, nak, nav, gk, gv = _in_projection(
        x2d, mod3, 0, batch, seq, lp["norm1_g"], lp["w_in_bf"], lp["head_gains"], latent=False)
    y_hy = _hyena(hy.reshape(batch, seq, HY_IN), lp)
    ona, og = _ctx_attention(qna, qg, nak, nav, gk, gv)
    x1, h2, aff = _out_projection(x2d, y_hy, ona, og, mod3, 0, seq, lp["norm2_g"], lp["w_out_bf"],
                                  lp["rw_hi"], lp["rw_lo"], latent=False)
    return x1, h2, aff, (nak, nav, gk, gv)


def _mixer_lat(x2d, batch, seq, mod3, lp, caches, layer):
    hy, qna, kna, vna, qg, kg, vg = _in_projection(
        x2d, mod3, 1, batch, seq, lp["norm1_g"], lp["w_in_bf"], lp["head_gains"], latent=True)
    y_hy = _hyena(hy.reshape(batch, seq, HY_IN), lp)
    na_kc, na_vc, g_kc, g_vc = caches
    ona = _na_latent(qna, kna, vna, na_kc, na_vc, layer, lp["na_rpb"], batch, seq)
    og = _gqa_latent(qg, kg, vg, g_kc, g_vc, layer, batch, seq)
    x1, h2, aff = _out_projection(x2d, y_hy, ona, og, mod3, 1, seq, lp["norm2_g"], lp["w_out_bf"],
                                  lp["rw_hi"], lp["rw_lo"], latent=True)
    return x1, h2, aff


def _moe(parts, mod3, lp):
    outs = []
    for x1, h2, aff, n_sets, n, mod_base, latent in parts:
        slot, slot_t, gate_t = _route(aff, n_sets, n)
        xg = _gather(slot, h2, n_sets, n)
        ne, _, cap, d = xg.shape
        y = _expert_ffn(xg.reshape(ne, n_sets * cap, d),
                        lp["exp_w_gate"], lp["exp_w_up"], lp["exp_w_down"], lp["layer"])
        outs.append(_scatter_residual(x1, slot_t, gate_t, y.reshape(xg.shape), mod3, mod_base, n_sets, n,
                                      latent))
    return outs


def kernel(x_prompt, x_sample, cache_na_k, cache_na_v, cache_gqa_k, cache_gqa_v, c, c_ctx, ada_w, ada_b, norm1_g, norm2_g, w_in, w_out, hy_short_w, hy_short_b, hy_f_w1, hy_f_b1, hy_f_w2, hy_f_b2, hy_f_w3, hy_f_freq, hy_bias, na_q_g, na_k_g, na_rpb, gqa_q_g, gqa_k_g, router_w, exp_w_gate, exp_w_up, exp_w_down):
    batch, seq, d = x_prompt.shape
    dbatch, dseq, _ = x_sample.shape
    depth = ada_w.shape[0]
    rows = 8 * ((1 + dbatch + 7) // 8)
    cvec = jnp.zeros((rows, d), F32).at[0].set(c_ctx).at[1:1 + dbatch].set(c)
    mod_all = _modulation(cvec, ada_w, ada_b)
    yp = x_prompt.reshape(batch * seq, d)
    ys = x_sample.reshape(dbatch * dseq, d)
    rw_t = jnp.swapaxes(router_w, 1, 2)
    rw_hi = rw_t.astype(BF16)
    rw_lo = (rw_t - rw_hi.astype(F32)).astype(BF16)
    new_kv = [[], [], [], []]
    for l in range(depth):
        lp = {
            "norm1_g": norm1_g[l], "norm2_g": norm2_g[l],
            "w_in_bf": w_in[l].astype(BF16), "w_out_bf": w_out[l].astype(BF16),
            "hy_short_w": hy_short_w[l], "hy_short_b": hy_short_b[l],
            "hy_f_w1": hy_f_w1[l], "hy_f_b1": hy_f_b1[l], "hy_f_w2": hy_f_w2[l], "hy_f_b2": hy_f_b2[l],
            "hy_f_w3": hy_f_w3[l], "hy_f_freq": hy_f_freq[l], "hy_bias": hy_bias[l],
            "head_gains": jnp.stack([na_q_g[l], na_k_g[l], gqa_q_g[l], gqa_k_g[l]]),
            "na_rpb": na_rpb[l], "rw_hi": rw_hi[l], "rw_lo": rw_lo[l],
            "layer": l, "exp_w_gate": exp_w_gate, "exp_w_up": exp_w_up, "exp_w_down": exp_w_down,
        }
        mod3 = mod_all[l].reshape(rows, 1, 6 * d)
        xp1, hp2, affp, kv = _mixer_ctx(yp, batch, seq, mod3, lp)
        for dst, src in zip(new_kv, kv):
            dst.append(src)
        xs1, hs2, affs = _mixer_lat(ys, dbatch, dseq, mod3, lp,
                                    (cache_na_k, cache_na_v, cache_gqa_k, cache_gqa_v), l)
        yp, ys = _moe([(xp1, hp2, affp, batch, seq, 0, False), (xs1, hs2, affs, dbatch, dseq, 1, True)],
                      mod3, lp)
    outs = [jnp.stack(v, axis=1) for v in new_kv]
    return (yp.reshape(batch, seq, d), ys.reshape(dbatch, dseq, d), outs[0], outs[1], outs[2], outs[3])
```

```python
import functools
import math

import numpy as np
import jax
import jax.numpy as jnp
from jax import lax
from jax.experimental import pallas as pl
from jax.experimental.pallas import tpu as pltpu

F32 = jnp.float32
BF16 = jnp.bfloat16

HEAD_DIM = 128
GRID_W = 64
HY_WIDTH = 512
HY_ORDER = 2
HY_IN = (HY_ORDER + 1) * HY_WIDTH
HY_POS_EMB = 33
HY_DECAY_TARGET = 1e-2
HY_FAST_PCT = 0.3
HY_SLOW_PCT = 1.5
NA_HEADS = 6
NA_WIN_H = 8
NA_WIN_W = 16
GQA_Q_HEADS = 6
GQA_KV_HEADS = 2
GQA_GROUP = GQA_Q_HEADS // GQA_KV_HEADS
ROPE_THETA = 10000.0
N_EXPERTS = 16
EC_CAPACITY_FACTOR = 2
NORM_EPS = 1e-6
MASK_VALUE = -1e30
NA_W = NA_HEADS * HEAD_DIM
GQ_W = GQA_Q_HEADS * HEAD_DIM
GKV_W = GQA_KV_HEADS * HEAD_DIM

LANES = 128
VMEM_LIMIT_BYTES = 56 * 1024 * 1024
LOG2_E = math.log2(math.e)
DFT_ROWS = 16
DFT_INNER = 128
DIRECT_DFT_MAX_LEN = 512


def _cp(n_axes, vmem=VMEM_LIMIT_BYTES):
    return pltpu.CompilerParams(dimension_semantics=("arbitrary",) * n_axes, vmem_limit_bytes=vmem)


def _nt(a, b):
    return lax.dot_general(a, b, (((1,), (1,)), ((), ())), preferred_element_type=F32)


def _mm(a, b):
    return jnp.dot(a, b, preferred_element_type=F32)


def _mm_hi(a, b):
    return jnp.dot(a, b, preferred_element_type=F32, precision=lax.Precision.HIGHEST)


def _mod_kernel(c_ref, w_ref, b_ref, o_ref):
    c = c_ref[...]
    s = c * jax.nn.sigmoid(c)
    o_ref[0] = _mm(s.astype(BF16), w_ref[0].astype(BF16)) + b_ref[0]


def _modulation(cvec, ada_w, ada_b):
    depth, d, n = ada_w.shape
    rows = cvec.shape[0]
    tn = min(n, 512)
    return pl.pallas_call(
        _mod_kernel,
        out_shape=jax.ShapeDtypeStruct((depth, rows, n), F32),
        grid=(depth, n // tn),
        in_specs=[
            pl.BlockSpec((rows, d), lambda l, j: (0, 0)),
            pl.BlockSpec((1, d, tn), lambda l, j: (l, 0, j)),
            pl.BlockSpec((1, 1, tn), lambda l, j: (l, 0, j)),
        ],
        out_specs=pl.BlockSpec((1, rows, tn), lambda l, j: (l, 0, j)),
        compiler_params=_cp(2),
        name="modulation",
    )(cvec, ada_w, ada_b.reshape(depth, 1, n))


def _head_norm(t, g):
    return t * lax.rsqrt(jnp.mean(t * t, axis=-1, keepdims=True) + NORM_EPS) * g


def _rope(t, cos, sin):
    swapped = jnp.where((lax.broadcasted_iota(jnp.int32, t.shape, 1) % 64) < 32,
                        pltpu.roll(t, HEAD_DIM - 32, 1), pltpu.roll(t, 32, 1))
    return t * cos + swapped * sin


def _inproj_kernel(latent, x_ref, mod_ref, g1_ref, w_ref, hg_ref, *rest):
    if latent:
        cos_ref, sin_ref, hy_ref, qna_ref, kna_ref, vna_ref, qg_ref, kg_ref, vg_ref = rest
    else:
        hy_ref, qna_ref, qg_ref, nak_ref, nav_ref, gk_ref, gv_ref = rest
    d = x_ref.shape[-1]
    x = x_ref[...]
    mod = mod_ref[0]
    shift, scale = mod[:, 0:d], mod[:, d:2 * d]
    ms = jnp.mean(x * x, axis=-1, keepdims=True)
    h = (x * lax.rsqrt(ms + NORM_EPS) * g1_ref[...]) * (1.0 + scale) + shift
    hb = h.astype(BF16)

    def proj(c0, n):
        return _mm(hb, w_ref[:, c0:c0 + n])

    hy_ref[...] = proj(0, HY_IN)
    q_scale = HEAD_DIM ** -0.5 * LOG2_E
    g_naq, g_nak = hg_ref[0:1, :], hg_ref[1:2, :]
    g_gq, g_gk = hg_ref[2:3, :], hg_ref[3:4, :]
    c = HY_IN
    z_naq = proj(c, NA_W)
    z_nak = proj(c + NA_W, NA_W)
    z_nav = proj(c + 2 * NA_W, NA_W)
    c += 3 * NA_W
    z_gq = proj(c, GQ_W)
    z_gk = proj(c + GQ_W, GKV_W)
    z_gv = proj(c + GQ_W + GKV_W, GKV_W)
    if latent:
        cos, sin = cos_ref[...], sin_ref[...]
    seq = None if latent else nak_ref.shape[2]
    nb = None if latent else nak_ref.shape[0]
    for hd in range(NA_HEADS):
        sl = slice(hd * HEAD_DIM, (hd + 1) * HEAD_DIM)
        qna_ref[:, sl] = (_head_norm(z_naq[:, sl], g_naq) * q_scale).astype(BF16)
        k = _head_norm(z_nak[:, sl], g_nak)
        v = z_nav[:, sl]
        if latent:
            kna_ref[:, sl] = k.astype(BF16)
            vna_ref[:, sl] = v.astype(BF16)
        else:
            for b in range(nb):
                nak_ref[b, hd] = k[b * seq:(b + 1) * seq]
                nav_ref[b, hd] = v[b * seq:(b + 1) * seq]
    for hd in range(GQA_Q_HEADS):
        sl = slice(hd * HEAD_DIM, (hd + 1) * HEAD_DIM)
        q = _head_norm(z_gq[:, sl], g_gq)
        if latent:
            q = _rope(q, cos, sin)
        qg_ref[:, sl] = (q * q_scale).astype(BF16)
    for hd in range(GQA_KV_HEADS):
        sl = slice(hd * HEAD_DIM, (hd + 1) * HEAD_DIM)
        k = _head_norm(z_gk[:, sl], g_gk)
        v = z_gv[:, sl]
        if latent:
            kg_ref[:, sl] = _rope(k, cos, sin).astype(BF16)
            vg_ref[:, sl] = v.astype(BF16)
        else:
            for b in range(nb):
                gk_ref[b, hd] = k[b * seq:(b + 1) * seq]
                gv_ref[b, hd] = v[b * seq:(b + 1) * seq]


def _rope_tables(n):
    pos = np.arange(n)
    row = (pos // GRID_W).astype(np.float64)
    col = (pos % GRID_W).astype(np.float64)
    quarter = HEAD_DIM // 4
    inv = ROPE_THETA ** (-np.arange(quarter, dtype=np.float64) / quarter)
    ang_r = row[:, None] * inv[None, :]
    ang_c = col[:, None] * inv[None, :]
    cos = np.concatenate([np.cos(ang_r), np.cos(ang_r), np.cos(ang_c), np.cos(ang_c)], axis=-1)
    sin = np.concatenate([-np.sin(ang_r), np.sin(ang_r), -np.sin(ang_c), np.sin(ang_c)], axis=-1)
    return jnp.asarray(cos, F32), jnp.asarray(sin, F32)


def _in_projection(x2d, mod3, mod_base, batch, seq, g1, w_in_bf, head_gains, latent):
    t, d = x2d.shape
    tm = min(512, seq) if latent else min(512, t)
    if not latent:
        tm = max(tm, seq)
    nt_ = t // tm
    tiles_per_batch = seq // tm if latent else None
    wide = w_in_bf.shape[1]
    if latent:
        mod_idx = lambda i: (mod_base + i // tiles_per_batch, 0, 0)
    else:
        mod_idx = lambda i: (mod_base, 0, 0)
    in_specs = [
        pl.BlockSpec((tm, d), lambda i: (i, 0)),
        pl.BlockSpec((1, 1, mod3.shape[-1]), mod_idx),
        pl.BlockSpec((1, d), lambda i: (0, 0)),
        pl.BlockSpec((d, wide), lambda i: (0, 0), pipeline_mode=pl.Buffered(1)),
        pl.BlockSpec((4, HEAD_DIM), lambda i: (0, 0)),
    ]
    args = [x2d, mod3, g1.reshape(1, d), w_in_bf, head_gains]
    row_spec = lambda w: pl.BlockSpec((tm, w), lambda i: (i, 0))
    if latent:
        cos, sin = _rope_tables(seq)
        in_specs += [pl.BlockSpec((tm, HEAD_DIM), lambda i: (i % tiles_per_batch, 0))] * 2
        args += [cos, sin]
        out_shape = [
            jax.ShapeDtypeStruct((t, HY_IN), F32),
            jax.ShapeDtypeStruct((t, NA_W), BF16), jax.ShapeDtypeStruct((t, NA_W), BF16),
            jax.ShapeDtypeStruct((t, NA_W), BF16), jax.ShapeDtypeStruct((t, GQ_W), BF16),
            jax.ShapeDtypeStruct((t, GKV_W), BF16), jax.ShapeDtypeStruct((t, GKV_W), BF16),
        ]
        out_specs = [row_spec(HY_IN), row_spec(NA_W), row_spec(NA_W), row_spec(NA_W),
                     row_spec(GQ_W), row_spec(GKV_W), row_spec(GKV_W)]
    else:
        nb = tm // seq
        kv_spec = lambda hh: pl.BlockSpec((nb, hh, seq, HEAD_DIM), lambda i: (i, 0, 0, 0))
        out_shape = [
            jax.ShapeDtypeStruct((t, HY_IN), F32),
            jax.ShapeDtypeStruct((t, NA_W), BF16), jax.ShapeDtypeStruct((t, GQ_W), BF16),
            jax.ShapeDtypeStruct((batch, NA_HEADS, seq, HEAD_DIM), F32),
            jax.ShapeDtypeStruct((batch, NA_HEADS, seq, HEAD_DIM), F32),
            jax.ShapeDtypeStruct((batch, GQA_KV_HEADS, seq, HEAD_DIM), F32),
            jax.ShapeDtypeStruct((batch, GQA_KV_HEADS, seq, HEAD_DIM), F32),
        ]
        out_specs = [row_spec(HY_IN), row_spec(NA_W), row_spec(GQ_W),
                     kv_spec(NA_HEADS), kv_spec(NA_HEADS), kv_spec(GQA_KV_HEADS), kv_spec(GQA_KV_HEADS)]
    return pl.pallas_call(
        functools.partial(_inproj_kernel, latent),
        out_shape=out_shape,
        grid=(nt_,),
        in_specs=in_specs,
        out_specs=out_specs,
        compiler_params=_cp(1),
        name="in_proj_lat" if latent else "in_proj_ctx",
    )(*args)


def _shortconv_kernel(x_ref, w_ref, b_ref, o_ref):
    o_ref[0, 0] = _short_conv_rows(x_ref[0], w_ref, b_ref)


def _short_conv(hy3, sw, sb):
    b, l, _ = hy3.shape
    tc = 256
    per = HY_WIDTH // tc
    return pl.pallas_call(
        _shortconv_kernel,
        out_shape=jax.ShapeDtypeStruct((HY_ORDER + 1, b, l, HY_WIDTH), F32),
        grid=(b, HY_IN // tc),
        in_specs=[
            pl.BlockSpec((1, l, tc), lambda i, q: (i, 0, q)),
            pl.BlockSpec((3, tc), lambda i, q: (0, q)),
            pl.BlockSpec((1, tc), lambda i, q: (0, q)),
        ],
        out_specs=pl.BlockSpec((1, 1, l, tc), lambda i, q: (q // per, i, 0, q % per)),
        compiler_params=_cp(2),
        name="hy_short_conv",
    )(hy3, sw, sb.reshape(1, HY_IN))


def _filter_kernel(z_ref, w1_ref, b1_ref, w2_ref, b2_ref, w3_ref, fr_ref, dl_ref, f_ref, n_ref):
    i = pl.program_id(0)
    z = z_ref[...]
    sf = fr_ref[...]
    hid = jnp.sin(sf * (_mm_hi(z, w1_ref[...]) + b1_ref[...]))
    hid = jnp.sin(sf * (_mm_hi(hid, w2_ref[...]) + b2_ref[...]))
    filt = _mm_hi(hid, w3_ref[...])
    t = z[:, 0:1]
    decay = jnp.exp(-t * dl_ref[...])
    decay = jnp.concatenate([decay] * (2 * HY_ORDER), axis=-1)
    filt = filt * decay
    half = HY_ORDER * HY_WIDTH
    rows = lax.broadcasted_iota(jnp.int32, filt.shape, 0) + i * filt.shape[0]
    cols = lax.broadcasted_iota(jnp.int32, filt.shape, 1)
    filt = jnp.where((rows == 0) & (cols >= half), 0.0, filt)
    f_ref[...] = filt
    part = jnp.sum(jnp.abs(filt), axis=0, keepdims=True)

    @pl.when(i == 0)
    def _():
        n_ref[...] = jnp.zeros_like(n_ref)

    n_ref[...] += part[:, :half] + part[:, half:]


def _filter_embedding(l):
    t = np.linspace(0.0, 1.0, l, dtype=np.float64)[:, None]
    bands = (HY_POS_EMB - 1) // 2
    fr = np.linspace(1e-4, bands - 1, bands, dtype=np.float64)[None, :]
    w = 2.0 * math.pi * np.arange(l, dtype=np.float64)[:, None] / l
    z = np.concatenate([t, np.cos(fr * w), -np.sin(fr * w)], axis=-1)
    zp = np.zeros((l, LANES), np.float32)
    zp[:, :HY_POS_EMB] = z
    min_d = abs(math.log(HY_DECAY_TARGET) / HY_SLOW_PCT)
    max_d = abs(math.log(HY_DECAY_TARGET) / HY_FAST_PCT)
    deltas = np.linspace(min_d, max_d, HY_WIDTH, dtype=np.float64)[None, :]
    return jnp.asarray(zp, F32), jnp.asarray(deltas, F32)


def _hyena_filters(l, w1, b1, w2, b2, w3, freq):
    zp, deltas = _filter_embedding(l)
    hid = w1.shape[1]
    w1p = jnp.zeros((LANES, hid), F32).at[:HY_POS_EMB].set(w1)
    tl = min(l, 512)
    wide = w3.shape[1]
    full = lambda a: pl.BlockSpec(a.shape, lambda i: (0,) * a.ndim)
    b1r, b2r, frr = b1.reshape(1, hid), b2.reshape(1, hid), freq.reshape(1, hid)
    return pl.pallas_call(
        _filter_kernel,
        out_shape=[jax.ShapeDtypeStruct((l, wide), F32),
                   jax.ShapeDtypeStruct((1, HY_ORDER * HY_WIDTH), F32)],
        grid=(l // tl,),
        in_specs=[pl.BlockSpec((tl, LANES), lambda i: (i, 0)), full(w1p), full(b1r), full(w2), full(b2r),
                  full(w3), full(frr), full(deltas)],
        out_specs=[pl.BlockSpec((tl, wide), lambda i: (i, 0)),
                   pl.BlockSpec((1, HY_ORDER * HY_WIDTH), lambda i: (0, 0))],
        compiler_params=_cp(1),
        name="hy_filters",
    )(zp, w1p, b1r, w2, b2r, w3, frr, deltas)


def _bf16_const(a):
    return jnp.asarray(a, F32).astype(BF16)


def _dft_plan(l):
    n = 2 * l
    n2_len = 1 if l <= DIRECT_DFT_MAX_LEN else DFT_INNER
    n1_len = n // n2_len
    k1 = np.arange(n1_len, dtype=np.float64)[:, None]
    n1 = np.arange(n1_len // 2, dtype=np.float64)[None, :]
    ang = 2.0 * np.pi * k1 * n1 / n1_len
    plan = {
        "n1": n1_len, "n2": n2_len,
        "fs": _bf16_const(np.concatenate([np.cos(ang), -np.sin(ang)], axis=0)),
        "gr": _bf16_const(np.cos(ang).T / n),
        "gi": _bf16_const(-np.sin(ang).T / n),
    }
    if n2_len > 1:
        k2 = np.arange(n2_len, dtype=np.float64)
        a2 = 2.0 * np.pi * np.outer(k2, k2) / n2_len
        fr, fi = np.cos(a2), -np.sin(a2)
        plan["d2"] = _bf16_const(np.block([[fr, -fi], [fi, fr]]))
        plan["d2c"] = _bf16_const(np.block([[fr, fi], [-fi, fr]]))
        eye = np.eye(DFT_ROWS)
        plan["fsk"] = _bf16_const(np.kron(np.concatenate([np.cos(ang), -np.sin(ang)], axis=0), eye))
        plan["grk"] = _bf16_const(np.kron(np.cos(ang).T / n, eye))
        plan["gik"] = _bf16_const(np.kron(-np.sin(ang).T / n, eye))
        at = 2.0 * np.pi * np.arange(n1_len, dtype=np.float64)[:, None] * k2[None, :] / n
        plan["twr"] = jnp.asarray(np.repeat(np.cos(at)[:, :, None], LANES, axis=2), F32)
        plan["twi"] = jnp.asarray(np.repeat(-np.sin(at)[:, :, None], LANES, axis=2), F32)
    return plan


def _dft1_kernel(z_ref, fs_ref, ar_ref, ai_ref):
    a = _mm(fs_ref[...], z_ref[0, 0].astype(BF16))
    n1 = ar_ref.shape[1]
    ar_ref[0] = a[:n1].astype(BF16)
    ai_ref[0] = a[n1:].astype(BF16)


def _dft1_rows_kernel(z_ref, fsk_ref, ar_ref, ai_ref):
    _, _, kk, mb, c = z_ref.shape
    n1 = ar_ref.shape[1]
    a = _mm(fsk_ref[...], z_ref[0, 0].reshape(kk * mb, c).astype(BF16))
    ar_ref[0] = a[:n1 * mb].reshape(n1, mb, c).astype(BF16)
    ai_ref[0] = a[n1 * mb:].reshape(n1, mb, c).astype(BF16)


def _dft_stage1(z, part, plan):
    n1 = plan["n1"]
    if plan["n2"] > 1:
        _, b, kk, n2, c = z.shape
        mb = DFT_ROWS
        tc = min(c, 512)
        return pl.pallas_call(
            _dft1_rows_kernel,
            out_shape=[jax.ShapeDtypeStruct((b, n1, n2, c), BF16)] * 2,
            grid=(b, n2 // mb, c // tc),
            in_specs=[pl.BlockSpec((1, 1, kk, mb, tc), lambda i, j, q: (part, i, 0, j, q)),
                      pl.BlockSpec(plan["fsk"].shape, lambda i, j, q: (0, 0))],
            out_specs=[pl.BlockSpec((1, n1, mb, tc), lambda i, j, q: (i, 0, j, q))] * 2,
            compiler_params=_cp(3),
            name="hy_dft_outer",
        )(z, plan["fsk"])
    z4 = z
    _, b, kk, nc = z4.shape
    tn = min(nc, 4096)
    return pl.pallas_call(
        _dft1_kernel,
        out_shape=[jax.ShapeDtypeStruct((b, n1, nc), BF16)] * 2,
        grid=(b, nc // tn),
        in_specs=[pl.BlockSpec((1, 1, kk, tn), lambda i, j: (part, i, 0, j)),
                  pl.BlockSpec((2 * n1, kk), lambda i, j: (0, 0))],
        out_specs=[pl.BlockSpec((1, n1, tn), lambda i, j: (i, 0, j))] * 2,
        compiler_params=_cp(2),
        name="hy_dft_outer",
    )(z4, plan["fs"])


def _twiddle_inner_dft(ar, ai, twr, twi, d2):
    c = ar.shape[-1]
    twr = jnp.concatenate([twr] * (c // LANES), axis=-1)
    twi = jnp.concatenate([twi] * (c // LANES), axis=-1)
    zr = ar * twr - ai * twi
    zi = ar * twi + ai * twr
    x = _mm(d2, jnp.concatenate([zr, zi], axis=0).astype(BF16))
    n2 = ar.shape[0]
    return x[:n2], x[n2:], twr, twi


def _spec_filter_kernel(two_stage, kb, ar_ref, ai_ref, nrm_ref, *rest):
    if two_stage:
        twr_ref, twi_ref, d2_ref, kr_ref, ki_ref = rest
    else:
        kr_ref, ki_ref = rest
    half = HY_ORDER * HY_WIDTH
    inv = 1.0 / nrm_ref[...]

    def combine(xr, xi):
        return (xr[:, :half] + xr[:, half:]) * inv, (xi[:, :half] - xi[:, half:]) * inv

    if two_stage:
        for j in range(kb):
            xr, xi, _, _ = _twiddle_inner_dft(ar_ref[0, j].astype(F32), ai_ref[0, j].astype(F32),
                                              twr_ref[j], twi_ref[j], d2_ref[...])
            kr, ki = combine(xr, xi)
            for o in range(HY_ORDER):
                kr_ref[o, j] = kr[:, o * HY_WIDTH:(o + 1) * HY_WIDTH]
                ki_ref[o, j] = ki[:, o * HY_WIDTH:(o + 1) * HY_WIDTH]
    else:
        kr, ki = combine(ar_ref[0].astype(F32), ai_ref[0].astype(F32))
        for o in range(HY_ORDER):
            kr_ref[o] = kr[:, o * HY_WIDTH:(o + 1) * HY_WIDTH]
            ki_ref[o] = ki[:, o * HY_WIDTH:(o + 1) * HY_WIDTH]


def _filter_spectrum(filt, norm, plan):
    l, wide = filt.shape
    n1, n2 = plan["n1"], plan["n2"]
    two_stage = n2 > 1
    zf = filt.reshape(1, 1, n1 // 2, n2, wide) if two_stage else filt.reshape(1, 1, n1 // 2, wide)
    ar, ai = _dft_stage1(zf, 0, plan)
    if two_stage:
        kb = 8
        a_spec = pl.BlockSpec((1, kb, n2, wide), lambda i: (0, i, 0, 0))
        in_specs = [a_spec, a_spec, pl.BlockSpec(norm.shape, lambda i: (0, 0)),
                    pl.BlockSpec((kb, n2, LANES), lambda i: (i, 0, 0)),
                    pl.BlockSpec((kb, n2, LANES), lambda i: (i, 0, 0)),
                    pl.BlockSpec((2 * n2, 2 * n2), lambda i: (0, 0))]
        args = [ar, ai, norm, plan["twr"], plan["twi"], plan["d2"]]
        out_shape = [jax.ShapeDtypeStruct((HY_ORDER, n1, n2, HY_WIDTH), F32)] * 2
        out_specs = [pl.BlockSpec((HY_ORDER, kb, n2, HY_WIDTH), lambda i: (0, i, 0, 0))] * 2
    else:
        kb = min(n1, 256)
        a_spec = pl.BlockSpec((1, kb, wide), lambda i: (0, i, 0))
        in_specs = [a_spec, a_spec, pl.BlockSpec(norm.shape, lambda i: (0, 0))]
        args = [ar, ai, norm]
        out_shape = [jax.ShapeDtypeStruct((HY_ORDER, n1, HY_WIDTH), F32)] * 2
        out_specs = [pl.BlockSpec((HY_ORDER, kb, HY_WIDTH), lambda i: (0, i, 0))] * 2
    return pl.pallas_call(
        functools.partial(_spec_filter_kernel, two_stage, kb),
        out_shape=out_shape,
        grid=(n1 // kb,),
        in_specs=in_specs,
        out_specs=out_specs,
        compiler_params=_cp(1),
        name="hy_filter_spectrum",
    )(*args)


def _spec_mul_kernel(kb, ar_ref, ai_ref, kr_ref, ki_ref, twr_ref, twi_ref, d2_ref, d2c_ref, br_ref, bi_ref):
    for j in range(kb):
        xr, xi, twr, twi = _twiddle_inner_dft(ar_ref[0, j].astype(F32), ai_ref[0, j].astype(F32),
                                              twr_ref[j], twi_ref[j], d2_ref[...])
        kr, ki = kr_ref[0, j], ki_ref[0, j]
        yr = xr * kr - xi * ki
        yi = xr * ki + xi * kr
        bm = _mm(d2c_ref[...], jnp.concatenate([yr, yi], axis=0).astype(BF16))
        n2 = xr.shape[0]
        br, bi = bm[:n2], bm[n2:]
        br_ref[0, j] = (br * twr + bi * twi).astype(BF16)
        bi_ref[0, j] = (bi * twr - br * twi).astype(BF16)


def _spectrum_multiply(ar, ai, kr, ki, order, plan):
    b, n1, n2, c = ar.shape
    kb = 8
    a_spec = pl.BlockSpec((1, kb, n2, c), lambda j, i: (i, j, 0, 0))
    k_spec = pl.BlockSpec((1, kb, n2, c), lambda j, i: (order, j, 0, 0))
    tw_spec = pl.BlockSpec((kb, n2, LANES), lambda j, i: (j, 0, 0))
    d_spec = pl.BlockSpec((2 * n2, 2 * n2), lambda j, i: (0, 0))
    return pl.pallas_call(
        functools.partial(_spec_mul_kernel, kb),
        out_shape=[jax.ShapeDtypeStruct((b, n1, n2, c), BF16)] * 2,
        grid=(n1 // kb, b),
        in_specs=[a_spec, a_spec, k_spec, k_spec, tw_spec, tw_spec, d_spec, d_spec],
        out_specs=[a_spec, a_spec],
        compiler_params=_cp(2),
        name="hy_spectrum_mul",
    )(ar, ai, kr, ki, plan["twr"], plan["twi"], plan["d2"], plan["d2c"])


def _idft1_rows_kernel(br_ref, bi_ref, grk_ref, gik_ref, z_ref, x_ref, bias_ref, o_ref):
    _, n1, mb, c = br_ref.shape
    conv = (_mm(grk_ref[...], br_ref[0].reshape(n1 * mb, c))
            + _mm(gik_ref[...], bi_ref[0].reshape(n1 * mb, c)))
    conv = conv.reshape(n1 // 2, mb, c)
    z = z_ref[0, 0]
    o_ref[0, 0] = (x_ref[0, 0] * (conv + z * bias_ref[...])).astype(o_ref.dtype)


def _idft_gate(br, bi, plan, z5, z_part, u5, x_part, bias_row, out_dtype):
    b, n1, n2, c = br.shape
    kk = n1 // 2
    mb = DFT_ROWS
    b_spec = pl.BlockSpec((1, n1, mb, c), lambda i, j: (i, 0, j, 0))
    g_spec = pl.BlockSpec(plan["grk"].shape, lambda i, j: (0, 0))
    return pl.pallas_call(
        _idft1_rows_kernel,
        out_shape=jax.ShapeDtypeStruct((1, b, kk, n2, c), out_dtype),
        grid=(b, n2 // mb),
        in_specs=[b_spec, b_spec, g_spec, g_spec,
                  pl.BlockSpec((1, 1, kk, mb, c), lambda i, j: (z_part, i, 0, j, 0)),
                  pl.BlockSpec((1, 1, kk, mb, c), lambda i, j: (x_part, i, 0, j, 0)),
                  pl.BlockSpec((1, c), lambda i, j: (0, 0))],
        out_specs=pl.BlockSpec((1, 1, kk, mb, c), lambda i, j: (0, i, 0, j, 0)),
        compiler_params=_cp(2),
        name="hy_idft_gate",
    )(br, bi, plan["grk"], plan["gik"], z5, u5, bias_row)


def _short_conv_rows(x, w_ref, b_ref):
    n = x.shape[0]
    rows = lax.broadcasted_iota(jnp.int32, x.shape, 0)
    prev = jnp.where(rows == 0, 0.0, pltpu.roll(x, 1, 0))
    nxt = jnp.where(rows == n - 1, 0.0, pltpu.roll(x, n - 1, 0))
    return prev * w_ref[0:1, :] + x * w_ref[1:2, :] + nxt * w_ref[2:3, :] + b_ref[...]


def _hyena_direct_kernel(hy_ref, sw_ref, sb_ref, fs_ref, gr_ref, gi_ref, kr_ref, ki_ref, hb_ref, o_ref):
    u = _short_conv_rows(hy_ref[0], sw_ref, sb_ref)
    nfreq = kr_ref.shape[1]
    z = u[:, 0:HY_WIDTH]
    for o in range(HY_ORDER):
        a = _mm(fs_ref[...], z.astype(BF16))
        ar, ai = a[:nfreq], a[nfreq:]
        kr, ki = kr_ref[o], ki_ref[o]
        br = (ar * kr - ai * ki).astype(BF16)
        bi = (ar * ki + ai * kr).astype(BF16)
        conv = _mm(gr_ref[...], br) + _mm(gi_ref[...], bi)
        z = u[:, (o + 1) * HY_WIDTH:(o + 2) * HY_WIDTH] * (conv + z * hb_ref[o:o + 1, :])
    o_ref[...] = z.astype(o_ref.dtype)


def _hyena_direct(hy3, lp, plan, kr, ki):
    b, l, wide = hy3.shape
    n1 = plan["n1"]
    full = lambda a: pl.BlockSpec(a.shape, lambda i: (0,) * a.ndim)
    sb = lp["hy_short_b"].reshape(1, wide)
    args = [lp["hy_short_w"], sb, plan["fs"], plan["gr"], plan["gi"], kr, ki, lp["hy_bias"]]
    return pl.pallas_call(
        _hyena_direct_kernel,
        out_shape=jax.ShapeDtypeStruct((b * l, HY_WIDTH), BF16),
        grid=(b,),
        in_specs=[pl.BlockSpec((1, l, wide), lambda i: (i, 0, 0))] + [full(a) for a in args],
        out_specs=pl.BlockSpec((l, HY_WIDTH), lambda i: (i, 0)),
        compiler_params=_cp(1),
        name="hy_direct",
    )(hy3, *args)


def _hyena(hy3, lp):
    b, l, _ = hy3.shape
    plan = _dft_plan(l)
    n1, n2 = plan["n1"], plan["n2"]
    filt, norm = _hyena_filters(l, lp["hy_f_w1"], lp["hy_f_b1"], lp["hy_f_w2"], lp["hy_f_b2"],
                                lp["hy_f_w3"], lp["hy_f_freq"])
    kr, ki = _filter_spectrum(filt, norm, plan)
    if n2 == 1:
        return _hyena_direct(hy3, lp, plan, kr, ki)
    u = _short_conv(hy3, lp["hy_short_w"], lp["hy_short_b"])
    u5 = u.reshape(HY_ORDER + 1, b, n1 // 2, n2, HY_WIDTH)
    z5, z_part = u5, 0
    for o in range(HY_ORDER):
        ar, ai = _dft_stage1(z5, z_part, plan)
        br, bi = _spectrum_multiply(ar, ai, kr, ki, o, plan)
        bias_row = lp["hy_bias"][o].reshape(1, HY_WIDTH)
        last = o == HY_ORDER - 1
        z5 = _idft_gate(br, bi, plan, z5, z_part, u5, o + 1, bias_row, BF16 if last else F32)
        z_part = 0
    return z5.reshape(b * l, HY_WIDTH)


def _softmax_pv(s_list, v_list):
    m = s_list[0].max(axis=-1, keepdims=True)
    for s in s_list[1:]:
        m = jnp.maximum(m, s.max(axis=-1, keepdims=True))
    den = 0.0
    o = 0.0
    for s, v in zip(s_list, v_list):
        e = jnp.exp2(s - m)
        den = den + e.sum(axis=-1, keepdims=True)
        o = o + _mm(e.astype(BF16), v)
    return o / den


def _ctx_attn_kernel(qna_ref, qg_ref, nak_ref, nav_ref, gk_ref, gv_ref, ona_ref, og_ref):
    for hd in range(NA_HEADS):
        sl = slice(hd * HEAD_DIM, (hd + 1) * HEAD_DIM)
        k = nak_ref[0, hd].astype(BF16)
        v = nav_ref[0, hd].astype(BF16)
        ona_ref[:, sl] = _softmax_pv([_nt(qna_ref[:, sl], k)], [v]).astype(BF16)
    for g in range(GQA_KV_HEADS):
        k = gk_ref[0, g].astype(BF16)
        v = gv_ref[0, g].astype(BF16)
        for r in range(GQA_GROUP):
            hd = g * GQA_GROUP + r
            sl = slice(hd * HEAD_DIM, (hd + 1) * HEAD_DIM)
            og_ref[:, sl] = _softmax_pv([_nt(qg_ref[:, sl], k)], [v]).astype(BF16)


def _ctx_attention(qna, qg, nak, nav, gk, gv):
    batch, _, seq, _ = nak.shape
    t = batch * seq
    kv = lambda hh: pl.BlockSpec((1, hh, seq, HEAD_DIM), lambda i: (i, 0, 0, 0))
    row = lambda w: pl.BlockSpec((seq, w), lambda i: (i, 0))
    return pl.pallas_call(
        _ctx_attn_kernel,
        out_shape=[jax.ShapeDtypeStruct((t, NA_W), BF16), jax.ShapeDtypeStruct((t, GQ_W), BF16)],
        grid=(batch,),
        in_specs=[row(NA_W), row(GQ_W), kv(NA_HEADS), kv(NA_HEADS), kv(GQA_KV_HEADS), kv(GQA_KV_HEADS)],
        out_specs=[row(NA_W), row(GQ_W)],
        compiler_params=_cp(1),
        name="ctx_attention",
    )(qna, qg, nak, nav, gk, gv)


NA_Q_ROWS = 4


def _na_tables(rows):
    r_q = NA_Q_ROWS
    kh = min(NA_WIN_H, rows)
    win = min(r_q + kh, rows)
    nblk = rows // r_q
    starts, var_ids, variants, keys = [], [], [], {}
    qr = np.arange(r_q)[:, None]
    kr = np.arange(win)[None, :]
    for j in range(nblk):
        start = int(np.clip(r_q * j - kh // 2, 0, rows - win))
        r = r_q * j + qr
        rs = np.clip(r - kh // 2, 0, rows - kh)
        kabs = start + kr
        row_ok = (kabs >= rs) & (kabs < rs + kh)
        ridx = np.where(row_ok, kabs - r + NA_WIN_H - 1, 0)
        key = ridx.tobytes() + row_ok.tobytes()
        if key not in keys:
            keys[key] = len(variants)
            variants.append((ridx, row_ok))
        starts.append(start)
        var_ids.append(keys[key])
    ridx = np.stack([v[0] for v in variants])
    row_ok = np.stack([v[1] for v in variants])
    qc = np.arange(GRID_W)[:, None]
    kc = np.arange(GRID_W)[None, :]
    cstart = np.clip(qc - NA_WIN_W // 2, 0, GRID_W - NA_WIN_W)
    col_ok = (kc >= cstart) & (kc < cstart + NA_WIN_W)
    return win, np.asarray(starts, np.int32), np.asarray(var_ids, np.int32), ridx, row_ok, col_ok


def _na_bias(rpb, ridx, row_ok, col_ok):
    h, nd, nrel = rpb.shape
    half = NA_WIN_W - 1
    period = 2 * GRID_W
    w = jnp.concatenate([rpb[..., half:], jnp.zeros((h, nd, period - nrel), F32), rpb[..., :half]], axis=-1)
    toep = jnp.tile(w, (1, 1, GRID_W))[..., :GRID_W * (period - 1)]
    toep = toep.reshape(h, nd, GRID_W, period - 1)[..., :GRID_W]
    toep = jnp.where(jnp.asarray(col_ok), toep * LOG2_E, MASK_VALUE)
    nvar, r_q, win = ridx.shape
    blocks = jnp.take(toep, jnp.asarray(ridx.reshape(-1)), axis=1)
    blocks = blocks.reshape(h, nvar, r_q, win, GRID_W, GRID_W)
    blocks = jnp.where(jnp.asarray(row_ok)[None, :, :, :, None, None], blocks, MASK_VALUE)
    return blocks.transpose(0, 1, 2, 4, 3, 5).reshape(h, nvar, r_q * GRID_W, win * GRID_W)


def _na_lat_kernel(win, start_ref, var_ref, q_ref, k_ref, v_ref, kc_ref, vc_ref, bias_ref, o_ref):
    j = pl.program_id(1)
    off = pl.multiple_of(start_ref[j] * GRID_W, GRID_W)
    for hd in range(NA_HEADS):
        sl = slice(hd * HEAD_DIM, (hd + 1) * HEAD_DIM)
        q = q_ref[:, sl]
        kw = k_ref[pl.ds(off, win * GRID_W), sl]
        vw = v_ref[pl.ds(off, win * GRID_W), sl]
        s_loc = _nt(q, kw) + bias_ref[hd, 0]
        s_ctx = _nt(q, kc_ref[0, 0, hd].astype(BF16))
        o_ref[:, sl] = _softmax_pv([s_loc, s_ctx], [vw, vc_ref[0, 0, hd].astype(BF16)]).astype(BF16)


def _na_latent(q, k, v, kc, vc, layer, rpb, batch, n):
    rows = n // GRID_W
    win, starts, var_ids, ridx, row_ok, col_ok = _na_tables(rows)
    bias = _na_bias(rpb, ridx, row_ok, col_ok)
    r_q = NA_Q_ROWS
    nblk = rows // r_q
    tq = r_q * GRID_W
    wk = win * GRID_W
    past = kc.shape[3]
    grid_spec = pltpu.PrefetchScalarGridSpec(
        num_scalar_prefetch=2,
        grid=(batch, nblk),
        in_specs=[
            pl.BlockSpec((tq, NA_W), lambda b, j, st, vr: (b * nblk + j, 0)),
            pl.BlockSpec((n, NA_W), lambda b, j, st, vr: (b, 0)),
            pl.BlockSpec((n, NA_W), lambda b, j, st, vr: (b, 0)),
            pl.BlockSpec((1, 1, NA_HEADS, past, HEAD_DIM), lambda b, j, st, vr: (b, layer, 0, 0, 0)),
            pl.BlockSpec((1, 1, NA_HEADS, past, HEAD_DIM), lambda b, j, st, vr: (b, layer, 0, 0, 0)),
            pl.BlockSpec((NA_HEADS, 1, tq, wk), lambda b, j, st, vr: (0, vr[j], 0, 0)),
        ],
        out_specs=pl.BlockSpec((tq, NA_W), lambda b, j, st, vr: (b * nblk + j, 0)),
    )
    return pl.pallas_call(
        functools.partial(_na_lat_kernel, win),
        out_shape=jax.ShapeDtypeStruct((batch * n, NA_W), BF16),
        grid_spec=grid_spec,
        compiler_params=_cp(2),
        name="na_latent",
    )(jnp.asarray(starts), jnp.asarray(var_ids), q, k, v, kc, vc, bias)


GQA_KEY_CHUNK = 512


def _gqa_lat_kernel(q_ref, k_ref, v_ref, kc_ref, vc_ref, o_ref, kall_ref, vall_ref, m_ref, acc_ref):
    tq = q_ref.shape[0]
    n = k_ref.shape[0]
    past = kc_ref.shape[3]
    total = n + past

    @pl.when(pl.program_id(2) == 0)
    def _():
        kall_ref[0:n, :] = k_ref[...]
        kall_ref[n:total, :] = kc_ref[0, 0, 0].astype(BF16)
        vall_ref[0:n, 0:HEAD_DIM] = v_ref[...]
        vall_ref[n:total, 0:HEAD_DIM] = vc_ref[0, 0, 0].astype(BF16)
        vall_ref[:, HEAD_DIM:2 * HEAD_DIM] = jnp.ones((total, HEAD_DIM), BF16)

    q = jnp.concatenate([q_ref[:, r * HEAD_DIM:(r + 1) * HEAD_DIM] for r in range(GQA_GROUP)], axis=0)
    m_ref[...] = jnp.full(m_ref.shape, -jnp.inf, F32)
    acc_ref[...] = jnp.zeros(acc_ref.shape, F32)

    def update(off, size):
        s = _nt(q, kall_ref[pl.ds(off, size), :])
        m_old = m_ref[...]
        m_new = jnp.maximum(m_old, s.max(axis=-1, keepdims=True))
        alpha = jnp.exp2(m_old - m_new)
        e = jnp.exp2(s - jnp.concatenate([m_new] * (size // LANES), axis=1))
        acc_ref[...] = (jnp.concatenate([alpha, alpha], axis=1) * acc_ref[...]
                        + _mm(e.astype(BF16), vall_ref[pl.ds(off, size), :]))
        m_ref[...] = m_new

    tk = min(GQA_KEY_CHUNK, total)
    full = total // tk

    def body(c, carry):
        update(pl.multiple_of(c * tk, tk), tk)
        return carry

    lax.fori_loop(0, full, body, 0, unroll=True)
    if total % tk:
        update(full * tk, total % tk)
    acc = acc_ref[...]
    o = acc[:, 0:HEAD_DIM] / acc[:, HEAD_DIM:2 * HEAD_DIM]
    for r in range(GQA_GROUP):
        o_ref[:, r * HEAD_DIM:(r + 1) * HEAD_DIM] = o[r * tq:(r + 1) * tq].astype(BF16)


def _gqa_latent(q, k, v, kc, vc, layer, batch, n):
    tq = min(256, n)
    nq = n // tq
    past = kc.shape[3]
    gw = GQA_GROUP * HEAD_DIM
    return pl.pallas_call(
        _gqa_lat_kernel,
        out_shape=jax.ShapeDtypeStruct((batch * n, GQ_W), BF16),
        grid=(batch, GQA_KV_HEADS, nq),
        in_specs=[
            pl.BlockSpec((tq, gw), lambda b, g, i: (b * nq + i, g)),
            pl.BlockSpec((n, HEAD_DIM), lambda b, g, i: (b, g)),
            pl.BlockSpec((n, HEAD_DIM), lambda b, g, i: (b, g)),
            pl.BlockSpec((1, 1, 1, past, HEAD_DIM), lambda b, g, i: (b, layer, g, 0, 0)),
            pl.BlockSpec((1, 1, 1, past, HEAD_DIM), lambda b, g, i: (b, layer, g, 0, 0)),
        ],
        out_specs=pl.BlockSpec((tq, gw), lambda b, g, i: (b * nq + i, g)),
        scratch_shapes=[pltpu.VMEM((n + past, HEAD_DIM), BF16), pltpu.VMEM((n + past, 2 * HEAD_DIM), BF16),
                        pltpu.VMEM((GQA_GROUP * tq, LANES), F32),
                        pltpu.VMEM((GQA_GROUP * tq, 2 * HEAD_DIM), F32)],
        compiler_params=_cp(3),
        name="gqa_latent",
    )(q, k, v, kc, vc)


def _outproj_kernel(x_ref, hy_ref, ona_ref, og_ref, mod_ref, g2_ref, wo_ref, rwh_ref, rwl_ref,
                    x1_ref, h2_ref, aff_ref):
    d = x_ref.shape[-1]
    mod = mod_ref[0]
    gate1 = mod[:, 2 * d:3 * d]
    shift2, scale2 = mod[:, 3 * d:4 * d], mod[:, 4 * d:5 * d]
    c1 = HY_WIDTH
    c2 = HY_WIDTH + NA_W
    c3 = c2 + GQ_W
    mixed = (_mm(hy_ref[...], wo_ref[0:c1, :]) + _mm(ona_ref[...], wo_ref[c1:c2, :])
             + _mm(og_ref[...], wo_ref[c2:c3, :]))
    x1 = x_ref[...] + gate1 * mixed
    x1_ref[...] = x1
    ms = jnp.mean(x1 * x1, axis=-1, keepdims=True)
    h2 = (x1 * lax.rsqrt(ms + NORM_EPS) * g2_ref[...]) * (1.0 + scale2) + shift2
    h2h = h2.astype(BF16)
    h2_ref[...] = h2h
    h2l = (h2 - h2h.astype(F32)).astype(BF16)
    logits = _nt(rwh_ref[...], h2h) + _nt(rwh_ref[...], h2l) + _nt(rwl_ref[...], h2h)
    m = logits.max(axis=0, keepdims=True)
    e = jnp.exp(logits - m)
    aff_ref[...] = e / e.sum(axis=0, keepdims=True)


def _out_projection(x2d, hy, ona, og, mod3, mod_base, seq, g2, w_out_bf, rw_hi, rw_lo, latent):
    t, d = x2d.shape
    tm = min(512, seq) if latent else min(512, t)
    tiles_per_batch = max(seq // tm, 1)
    if latent:
        mod_idx = lambda i: (mod_base + i // tiles_per_batch, 0, 0)
    else:
        mod_idx = lambda i: (mod_base, 0, 0)
    row = lambda w: pl.BlockSpec((tm, w), lambda i: (i, 0))
    ne = rw_hi.shape[0]
    return pl.pallas_call(
        _outproj_kernel,
        out_shape=[jax.ShapeDtypeStruct((t, d), F32), jax.ShapeDtypeStruct((t, d), BF16),
                   jax.ShapeDtypeStruct((ne, t), F32)],
        grid=(t // tm,),
        in_specs=[row(d), row(HY_WIDTH), row(NA_W), row(GQ_W),
                  pl.BlockSpec((1, 1, mod3.shape[-1]), mod_idx),
                  pl.BlockSpec((1, d), lambda i: (0, 0)),
                  pl.BlockSpec(w_out_bf.shape, lambda i: (0, 0), pipeline_mode=pl.Buffered(1)),
                  pl.BlockSpec((ne, d), lambda i: (0, 0)),
                  pl.BlockSpec((ne, d), lambda i: (0, 0))],
        out_specs=[row(d), row(d), pl.BlockSpec((ne, tm), lambda i: (0, i))],
        compiler_params=_cp(1),
        name="out_proj_lat" if latent else "out_proj_ctx",
    )(x2d, hy, ona, og, mod3, g2.reshape(1, d), w_out_bf, rw_hi, rw_lo)


ROUTE_CHUNK = 256


def _prefix_count(mask_f, tri):
    ne, n = mask_f.shape
    ch = tri.shape[0]
    pieces = []
    carry = jnp.zeros((ne, 1), F32)
    for c in range(n // ch):
        mk = mask_f[:, c * ch:(c + 1) * ch]
        inc = _mm(mk.astype(BF16), tri)
        pieces.append(inc - mk + carry)
        carry = carry + inc[:, ch - 1:ch]
    return (pieces[0] if len(pieces) == 1 else jnp.concatenate(pieces, axis=1)), carry


def _route_kernel(cap, n, aff_ref, slot_ref, slott_ref, gatet_ref):
    ne = aff_ref.shape[0]
    sets = aff_ref.shape[1] // n
    affs = [aff_ref[:, s * n:(s + 1) * n] for s in range(sets)]

    def bisect(i, thr_bits):
        out = []
        for aff, bits in zip(affs, thr_bits):
            cand = bits | (jnp.int32(1) << (30 - i))
            cnt = jnp.sum(jnp.where(aff >= pltpu.bitcast(cand, F32), 1.0, 0.0), axis=1, keepdims=True)
            out.append(jnp.where(cnt >= cap, cand, bits))
        return tuple(out)

    thr_bits = lax.fori_loop(0, 31, bisect, tuple(jnp.zeros((ne, 1), jnp.int32) for _ in range(sets)))
    ch = min(ROUTE_CHUNK, n)
    tri = jnp.where(lax.broadcasted_iota(jnp.int32, (ch, ch), 0) <= lax.broadcasted_iota(jnp.int32, (ch, ch), 1),
                    1.0, 0.0).astype(BF16)
    pad = jnp.zeros((LANES - ne, ch), F32)
    for s, aff in enumerate(affs):
        thr = pltpu.bitcast(thr_bits[s], F32)
        gt = jnp.where(aff > thr, 1.0, 0.0)
        eq = jnp.where(aff == thr, 1.0, 0.0)
        n_gt = jnp.sum(gt, axis=1, keepdims=True)
        eq_rank, _ = _prefix_count(eq, tri)
        sel = gt + eq * (eq_rank < (cap - n_gt)).astype(F32)
        rank, _ = _prefix_count(sel, tri)
        slot = jnp.where(sel > 0.0, rank, -1.0).astype(jnp.int32)
        slot_ref[:, s * n:(s + 1) * n] = slot
        for c in range(n // ch):
            r0 = s * n + c * ch
            blk = jnp.concatenate([slot[:, c * ch:(c + 1) * ch].astype(F32), pad], axis=0)
            slott_ref[r0:r0 + ch, :] = blk.T.astype(jnp.int32)
            gblk = jnp.concatenate([aff[:, c * ch:(c + 1) * ch], pad], axis=0)
            gatet_ref[r0:r0 + ch, :] = gblk.T


ROUTE_BLOCK_TOKENS = 2048


def _route(aff_t, n_sets, n):
    ne, t = aff_t.shape
    cap = EC_CAPACITY_FACTOR * n // N_EXPERTS
    sets = min(n_sets, max(1, ROUTE_BLOCK_TOKENS // n))
    blk = sets * n
    return pl.pallas_call(
        functools.partial(_route_kernel, cap, n),
        out_shape=[jax.ShapeDtypeStruct((ne, t), jnp.int32), jax.ShapeDtypeStruct((t, LANES), jnp.int32),
                   jax.ShapeDtypeStruct((t, LANES), F32)],
        grid=(n_sets // sets,),
        in_specs=[pl.BlockSpec((ne, blk), lambda b: (0, b))],
        out_specs=[pl.BlockSpec((ne, blk), lambda b: (0, b)), pl.BlockSpec((blk, LANES), lambda b: (b, 0)),
                   pl.BlockSpec((blk, LANES), lambda b: (b, 0))],
        compiler_params=_cp(1),
        name="route",
    )(aff_t)


GATHER_ROWS = 512
MOE_SMALL_SET = 512


def _gather_kernel(slot_ref, h_ref, o_ref):
    eb, _, cap, _ = o_ref.shape
    n = slot_ref.shape[-1]
    srow = lax.broadcasted_iota(jnp.int32, (cap, n), 0)
    onehot = [jnp.where(slot_ref[e] == srow, 1.0, 0.0).astype(BF16) for e in range(eb)]
    onehot = onehot[0] if eb == 1 else jnp.concatenate(onehot, axis=0)
    xg = _mm(onehot, h_ref[...]).astype(BF16)
    for e in range(eb):
        o_ref[e, 0] = xg[e * cap:(e + 1) * cap]


def _gather(slot, h2, n_sets, n):
    ne, t = slot.shape
    d = h2.shape[1]
    cap = EC_CAPACITY_FACTOR * n // N_EXPERTS
    eb = min(ne, max(1, GATHER_ROWS // cap))
    td = d if n <= MOE_SMALL_SET else min(d, 512)
    slot3 = slot.reshape(ne, 1, t)
    return pl.pallas_call(
        _gather_kernel,
        out_shape=jax.ShapeDtypeStruct((ne, n_sets, cap, d), BF16),
        grid=(n_sets, d // td, ne // eb),
        in_specs=[pl.BlockSpec((eb, 1, n), lambda b, j, e: (e, 0, b)),
                  pl.BlockSpec((n, td), lambda b, j, e: (b, j))],
        out_specs=pl.BlockSpec((eb, 1, cap, td), lambda b, j, e: (e, b, 0, j)),
        compiler_params=_cp(3),
        name="moe_gather",
    )(slot3, h2)


def _ffn_kernel(x_ref, wg_ref, wu_ref, wd_ref, y_ref, acc_ref):
    f = pl.program_id(2)

    @pl.when(f == 0)
    def _():
        acc_ref[...] = jnp.zeros_like(acc_ref)

    x = x_ref[0]
    a = _mm(x, wg_ref[0, 0].astype(BF16))
    u = _mm(x, wu_ref[0, 0].astype(BF16))
    hmid = (a * jax.nn.sigmoid(a) * u).astype(BF16)
    acc_ref[...] += _mm(hmid, wd_ref[0, 0].astype(BF16))

    @pl.when(f == pl.num_programs(2) - 1)
    def _():
        y_ref[0] = acc_ref[...].astype(BF16)


FFN_ROWS = 1024
FFN_COLS = 512


def _expert_ffn(xg, w_gate, w_up, w_down, layer):
    ne, m, d = xg.shape
    ff = w_gate.shape[-1]
    tf = min(ff, FFN_COLS)
    tm = min(m, FFN_ROWS)
    return pl.pallas_call(
        _ffn_kernel,
        out_shape=jax.ShapeDtypeStruct((ne, m, d), BF16),
        grid=(ne, m // tm, ff // tf),
        in_specs=[pl.BlockSpec((1, tm, d), lambda e, i, f: (e, i, 0)),
                  pl.BlockSpec((1, 1, d, tf), lambda e, i, f: (layer, e, 0, f)),
                  pl.BlockSpec((1, 1, d, tf), lambda e, i, f: (layer, e, 0, f)),
                  pl.BlockSpec((1, 1, tf, d), lambda e, i, f: (layer, e, f, 0))],
        out_specs=pl.BlockSpec((1, tm, d), lambda e, i, f: (e, i, 0)),
        scratch_shapes=[pltpu.VMEM((tm, d), F32)],
        compiler_params=_cp(3),
        name="moe_ffn",
    )(xg, w_gate, w_up, w_down)


def _scatter_kernel(fused, x_ref, slott_ref, gatet_ref, y_ref, gate2_ref, *rest):
    td = x_ref.shape[-1]
    ne, _, cap, _ = y_ref.shape
    tn = x_ref.shape[0]
    st = slott_ref[:, 0:ne]
    gt = gatet_ref[:, 0:ne]
    if fused:
        expand_ref, target_ref, o_ref = rest
        expand = expand_ref[...]
        slot_k = _mm(st.astype(F32).astype(BF16), expand)
        hit = slot_k == target_ref[...]
        g_hi = gt.astype(BF16)
        g_lo = (gt - g_hi.astype(F32)).astype(BF16)
        y_all = jnp.concatenate([y_ref[e, 0] for e in range(ne)], axis=0)
        moe = (_mm(jnp.where(hit, _mm(g_hi, expand), 0.0).astype(BF16), y_all)
               + _mm(jnp.where(hit, _mm(g_lo, expand), 0.0).astype(BF16), y_all))
    else:
        (o_ref,) = rest
        lane = lax.broadcasted_iota(jnp.int32, (tn, cap), 1)
        moe = jnp.zeros((tn, td), F32)
        for e in range(ne):
            onehot = jnp.where(st[:, e:e + 1] == lane, 1.0, 0.0).astype(BF16)
            moe = moe + gt[:, e:e + 1] * _mm(onehot, y_ref[e, 0])
    o_ref[...] = x_ref[...] + gate2_ref[0] * moe


SCATTER_FUSED_K = 512


def _scatter_residual(x1, slot_t, gate_t, y4, mod3, mod_base, n_sets, n, latent):
    t, d = x1.shape
    ne, _, cap, _ = y4.shape
    tn = min(n, 256)
    td = d if n <= MOE_SMALL_SET else min(d, 512)
    npt = n // tn
    g2_blk = 5 * (d // td)
    if latent:
        mod_idx = lambda b, j, i: (mod_base + b, 0, g2_blk + j)
    else:
        mod_idx = lambda b, j, i: (mod_base, 0, g2_blk + j)
    in_specs = [pl.BlockSpec((tn, td), lambda b, j, i: (b * npt + i, j)),
                pl.BlockSpec((tn, LANES), lambda b, j, i: (b * npt + i, 0)),
                pl.BlockSpec((tn, LANES), lambda b, j, i: (b * npt + i, 0)),
                pl.BlockSpec((ne, 1, cap, td), lambda b, j, i: (0, b, 0, j)),
                pl.BlockSpec((1, 1, td), mod_idx)]
    args = [x1, slot_t, gate_t, y4, mod3]
    k = ne * cap
    fused = k <= SCATTER_FUSED_K
    if fused:
        cols = np.arange(k)
        expand = (cols[None, :] // cap == np.arange(ne)[:, None]).astype(np.float32)
        in_specs += [pl.BlockSpec((ne, k), lambda b, j, i: (0, 0)), pl.BlockSpec((1, k), lambda b, j, i: (0, 0))]
        args += [jnp.asarray(expand, F32).astype(BF16), jnp.asarray((cols % cap)[None, :], F32)]
    return pl.pallas_call(
        functools.partial(_scatter_kernel, fused),
        out_shape=jax.ShapeDtypeStruct((t, d), F32),
        grid=(n_sets, d // td, npt),
        in_specs=in_specs,
        out_specs=pl.BlockSpec((tn, td), lambda b, j, i: (b * npt + i, j)),
        compiler_params=_cp(3),
        name="moe_scatter",
    )(*args)


def _mixer_ctx(x2d, batch, seq, mod3, lp):
    hy, qna, qg, nak, nav, gk, gv = _in_projection(
        x2d, mod3, 0, batch, seq, lp["norm1_g"], lp["w_in_bf"], lp["head_gains"], latent=False)
    y_hy = _hyena(hy.reshape(batch, seq, HY_IN), lp)
    ona, og = _ctx_attention(qna, qg, nak, nav, gk, gv)
    x1, h2, aff = _out_projection(x2d, y_hy, ona, og, mod3, 0, seq, lp["norm2_g"], lp["w_out_bf"],
                                  lp["rw_hi"], lp["rw_lo"], latent=False)
    return x1, h2, aff, (nak, nav, gk, gv)


def _mixer_lat(x2d, batch, seq, mod3, lp, caches, layer):
    hy, qna, kna, vna, qg, kg, vg = _in_projection(
        x2d, mod3, 1, batch, seq, lp["norm1_g"], lp["w_in_bf"], lp["head_gains"], latent=True)
    y_hy = _hyena(hy.reshape(batch, seq, HY_IN), lp)
    na_kc, na_vc, g_kc, g_vc = caches
    ona = _na_latent(qna, kna, vna, na_kc, na_vc, layer, lp["na_rpb"], batch, seq)
    og = _gqa_latent(qg, kg, vg, g_kc, g_vc, layer, batch, seq)
    x1, h2, aff = _out_projection(x2d, y_hy, ona, og, mod3, 1, seq, lp["norm2_g"], lp["w_out_bf"],
                                  lp["rw_hi"], lp["rw_lo"], latent=True)
    return x1, h2, aff


def _moe(parts, mod3, lp):
    outs = []
    for x1, h2, aff, n_sets, n, mod_base, latent in parts:
        slot, slot_t, gate_t = _route(aff, n_sets, n)
        xg = _gather(slot, h2, n_sets, n)
        ne, _, cap, d = xg.shape
        y = _expert_ffn(xg.reshape(ne, n_sets * cap, d),
                        lp["exp_w_gate"], lp["exp_w_up"], lp["exp_w_down"], lp["layer"])
        outs.append(_scatter_residual(x1, slot_t, gate_t, y.reshape(xg.shape), mod3, mod_base, n_sets, n,
                                      latent))
    return outs


def kernel(x_prompt, x_sample, cache_na_k, cache_na_v, cache_gqa_k, cache_gqa_v, c, c_ctx, ada_w, ada_b, norm1_g, norm2_g, w_in, w_out, hy_short_w, hy_short_b, hy_f_w1, hy_f_b1, hy_f_w2, hy_f_b2, hy_f_w3, hy_f_freq, hy_bias, na_q_g, na_k_g, na_rpb, gqa_q_g, gqa_k_g, router_w, exp_w_gate, exp_w_up, exp_w_down):
    batch, seq, d = x_prompt.shape
    dbatch, dseq, _ = x_sample.shape
    depth = ada_w.shape[0]
    rows = 8 * ((1 + dbatch + 7) // 8)
    cvec = jnp.zeros((rows, d), F32).at[0].set(c_ctx).at[1:1 + dbatch].set(c)
    mod_all = _modulation(cvec, ada_w, ada_b)
    yp = x_prompt.reshape(batch * seq, d)
    ys = x_sample.reshape(dbatch * dseq, d)
    rw_t = jnp.swapaxes(router_w, 1, 2)
    rw_hi = rw_t.astype(BF16)
    rw_lo = (rw_t - rw_hi.astype(F32)).astype(BF16)
    new_kv = [[], [], [], []]
    for l in range(depth):
        lp = {
            "norm1_g": norm1_g[l], "norm2_g": norm2_g[l],
            "w_in_bf": w_in[l].astype(BF16), "w_out_bf": w_out[l].astype(BF16),
            "hy_short_w": hy_short_w[l], "hy_short_b": hy_short_b[l],
            "hy_f_w1": hy_f_w1[l], "hy_f_b1": hy_f_b1[l], "hy_f_w2": hy_f_w2[l], "hy_f_b2": hy_f_b2[l],
            "hy_f_w3": hy_f_w3[l], "hy_f_freq": hy_f_freq[l], "hy_bias": hy_bias[l],
            "head_gains": jnp.stack([na_q_g[l], na_k_g[l], gqa_q_g[l], gqa_k_g[l]]),
            "na_rpb": na_rpb[l], "rw_hi": rw_hi[l], "rw_lo": rw_lo[l],
            "layer": l, "exp_w_gate": exp_w_gate, "exp_w_up": exp_w_up, "exp_w_down": exp_w_down,
        }
        mod3 = mod_all[l].reshape(rows, 1, 6 * d)
        xp1, hp2, affp, kv = _mixer_ctx(yp, batch, seq, mod3, lp)
        for dst, src in zip(new_kv, kv):
            dst.append(src)
        xs1, hs2, affs = _mixer_lat(ys, dbatch, dseq, mod3, lp,
                                    (cache_na_k, cache_na_v, cache_gqa_k, cache_gqa_v), l)
        yp, ys = _moe([(xp1, hp2, affp, batch, seq, 0, False), (xs1, hs2, affs, dbatch, dseq, 1, True)],
                      mod3, lp)
    outs = [jnp.stack(v, axis=1) for v in new_kv]
    return (yp.reshape(batch, seq, d), ys.reshape(dbatch, dseq, d), outs[0], outs[1], outs[2], outs[3])
```
